```python
import math
import jax
import jax.numpy as jnp
from jax import lax
import numpy as np

D_MODEL = 1024
BATCH = 2
SEQ = 8192
DEPTH = 2

HEAD_DIM = 64
SB_HEADS = 8
SB_WIDTH = SB_HEADS * HEAD_DIM
CONV_GROUPS = 8
CONV_WIDTH = D_MODEL - SB_WIDTH
CONV_GROUP_DIM = CONV_WIDTH // CONV_GROUPS
CONV_K = 3
EVEN_IN = 3 * SB_WIDTH + 3 * CONV_WIDTH
DIFF_HEADS = 8
DIFF_QK_DIM = 64
DIFF_V_DIM = 2 * DIFF_QK_DIM
DIFF_WIDTH = DIFF_HEADS * DIFF_V_DIM
ODD_IN = DIFF_HEADS * (4 * DIFF_QK_DIM + DIFF_V_DIM)
D_FF = 4 * D_MODEL
N_EVEN = (DEPTH + 1) // 2
N_ODD = DEPTH // 2
Q_BLOCK = 128
NORM_EPS = 1e-6

kernel_name = 'hybrid_stickbreak_shortconv_diffattn'


def rmsnorm(x, g):
    xf = x.astype(jnp.float32)
    y = xf * lax.rsqrt(jnp.mean(jnp.square(xf), axis=-1, keepdims=True) + NORM_EPS)
    return (y * g.astype(jnp.float32)).astype(x.dtype)


def split_heads(t, n_heads):
    b, s, _ = t.shape
    return t.reshape(b, s, n_heads, -1).transpose(0, 2, 1, 3)


def merge_heads(t):
    b, h, s, d = t.shape
    return t.transpose(0, 2, 1, 3).reshape(b, s, h * d)


def stick_breaking_attention(q, k, v):
    seq = q.shape[2]
    scale = HEAD_DIM ** -0.5
    outs = []
    for start in range(0, seq, Q_BLOCK):
        end = start + Q_BLOCK
        z = jnp.einsum('bhqd,bhkd->bhqk', q[:, :, start:end], k[:, :, :end]).astype(jnp.float32) * scale
        past = jnp.arange(end)[None, :] < jnp.arange(start, end)[:, None]
        log_beta = jax.nn.log_sigmoid(z)
        log_keep = jnp.where(past, log_beta - z, 0.0)
        later = lax.cumsum(log_keep, axis=3, reverse=True) - log_keep
        w = jnp.where(past, jnp.exp(log_beta + later), 0.0)
        outs.append(jnp.einsum('bhqk,bhkd->bhqd', w.astype(v.dtype), v[:, :, :end]))
    return jnp.concatenate(outs, axis=2)


def short_gated_conv(b_gate, c_gate, u, w):
    seq = u.shape[1]
    cu = c_gate * u
    padded = jnp.pad(cu, ((0, 0), (CONV_K - 1, 0), (0, 0)))
    y = sum(padded[:, j:j + seq] * w[j] for j in range(CONV_K))
    return b_gate * y


def differential_attention(q1, q2, k1, k2, v, lam):
    seq = q1.shape[2]
    scale = DIFF_QK_DIM ** -0.5
    lam = lam.astype(jnp.float32)
    outs = []
    for start in range(0, seq, Q_BLOCK):
        end = start + Q_BLOCK
        causal = jnp.arange(end)[None, :] <= jnp.arange(start, end)[:, None]
        s1 = jnp.einsum('bhqd,bhkd->bhqk', q1[:, :, start:end], k1[:, :, :end]).astype(jnp.float32) * scale
        s2 = jnp.einsum('bhqd,bhkd->bhqk', q2[:, :, start:end], k2[:, :, :end]).astype(jnp.float32) * scale
        p1 = jax.nn.softmax(jnp.where(causal, s1, -jnp.inf), axis=-1)
        p2 = jax.nn.softmax(jnp.where(causal, s2, -jnp.inf), axis=-1)
        w = p1 - lam * p2
        outs.append(jnp.einsum('bhqk,bhkd->bhqd', w.astype(v.dtype), v[:, :, :end]))
    return jnp.concatenate(outs, axis=2)


def setup_inputs(seed: int = 0) -> dict:
    key = jax.random.key(seed)
    ks = jax.random.split(key, 16)

    def normal(k, shape, scale):
        return jax.random.normal(k, shape, jnp.float32) * scale

    return {
        'x': normal(ks[0], (BATCH, SEQ, D_MODEL), 1.0),
        'norm_mix': 1.0 + normal(ks[1], (DEPTH, D_MODEL), 0.02),
        'norm_mlp': 1.0 + normal(ks[2], (DEPTH, D_MODEL), 0.02),
        'norm_final': 1.0 + normal(ks[3], (D_MODEL,), 0.02),
        'w_in_even': normal(ks[4], (N_EVEN, D_MODEL, EVEN_IN), D_MODEL ** -0.5),
        'conv_w': normal(ks[5], (N_EVEN, CONV_K, CONV_WIDTH), CONV_K ** -0.5),
        'w_out_even': normal(ks[6], (N_EVEN, SB_WIDTH + CONV_WIDTH, D_MODEL), (SB_WIDTH + CONV_WIDTH) ** -0.5),
        'w_in_odd': normal(ks[7], (N_ODD, D_MODEL, ODD_IN), D_MODEL ** -0.5),
        'lam_q1': normal(ks[8], (N_ODD, DIFF_QK_DIM), 0.1),
        'lam_k1': normal(ks[9], (N_ODD, DIFF_QK_DIM), 0.1),
        'lam_q2': normal(ks[10], (N_ODD, DIFF_QK_DIM), 0.1),
        'lam_k2': normal(ks[11], (N_ODD, DIFF_QK_DIM), 0.1),
        'subln_g': 1.0 + normal(ks[12], (N_ODD, DIFF_V_DIM), 0.02),
        'w_out_odd': normal(ks[13], (N_ODD, DIFF_WIDTH, D_MODEL), DIFF_WIDTH ** -0.5),
        'w_up': normal(ks[14], (DEPTH, D_MODEL, D_FF), D_MODEL ** -0.5),
        'w_down': normal(ks[15], (DEPTH, D_FF, D_MODEL), D_FF ** -0.5),
    }


def reference(x, norm_mix, norm_mlp, norm_final, w_in_even, conv_w, w_out_even,
              w_in_odd, lam_q1, lam_k1, lam_q2, lam_k2, subln_g, w_out_odd, w_up, w_down):
    b, s, _ = x.shape
    for layer in range(DEPTH):
        i = layer // 2
        h = rmsnorm(x, norm_mix[layer])
        if layer % 2 == 0:
            proj = h @ w_in_even[i]
            q, k, v, b_gate, c_gate, u = jnp.split(
                proj,
                [SB_WIDTH, 2 * SB_WIDTH, 3 * SB_WIDTH,
                 3 * SB_WIDTH + CONV_WIDTH, 3 * SB_WIDTH + 2 * CONV_WIDTH],
                axis=-1)
            a_out = merge_heads(stick_breaking_attention(
                split_heads(q, SB_HEADS), split_heads(k, SB_HEADS), split_heads(v, SB_HEADS)))
            c_out = short_gated_conv(b_gate, c_gate, u, conv_w[i])
            mix = jnp.concatenate([a_out, c_out], axis=-1) @ w_out_even[i]
        else:
            proj = h @ w_in_odd[i]
            qk_w = DIFF_HEADS * 2 * DIFF_QK_DIM
            q, k, v = jnp.split(proj, [qk_w, 2 * qk_w], axis=-1)
            q = q.reshape(b, s, DIFF_HEADS, 2, DIFF_QK_DIM).transpose(0, 2, 3, 1, 4)
            k = k.reshape(b, s, DIFF_HEADS, 2, DIFF_QK_DIM).transpose(0, 2, 3, 1, 4)
            lambda_init = 0.8 - 0.6 * math.exp(-0.3 * layer)
            lam = (jnp.exp(jnp.sum(lam_q1[i] * lam_k1[i]))
                   - jnp.exp(jnp.sum(lam_q2[i] * lam_k2[i])) + lambda_init)
            o = differential_attention(q[:, :, 0], q[:, :, 1], k[:, :, 0], k[:, :, 1],
                                       split_heads(v, DIFF_HEADS), lam)
            o = rmsnorm(o, subln_g[i]) * (1.0 - lambda_init)
            mix = merge_heads(o) @ w_out_odd[i]
        x = x + mix
        h = rmsnorm(x, norm_mlp[layer])
        x = x + jnp.square(jax.nn.relu(h @ w_up[layer])) @ w_down[layer]
    return rmsnorm(x, norm_final)
```

```python
import functools
import math

import jax
import jax.numpy as jnp
from jax import lax
from jax.experimental import pallas as pl
from jax.experimental.pallas import tpu as pltpu

F32 = jnp.float32
BF16 = jnp.bfloat16

NORM_EPS = 1e-6
HEAD_DIM = 64
SB_HEADS = 8
SB_WIDTH = SB_HEADS * HEAD_DIM
CONV_K = 3
DIFF_HEADS = 8
DIFF_V_DIM = 2 * HEAD_DIM
QK_SCALE = HEAD_DIM ** -0.5

LANES = 128
SUBLANES = 8
VMEM_LIMIT = 48 * 1024 * 1024

ROW_TILE = 512
FF_CHUNK = 1024
SB_BLOCK = 128
DIFF_BLOCK = 256

SB_SKIP_LOG_MASS = 50.0


def _rmsnorm_rows(x, g):
    ms = jnp.mean(x * x, axis=-1, keepdims=True)
    return x * lax.rsqrt(ms + NORM_EPS) * g


def _whole(shape):
    return pl.BlockSpec(shape, lambda *_: (0,) * len(shape), pipeline_mode=pl.Buffered(1))


def _in_proj_even_kernel(x_ref, g_ref, w_ref, cw_ref, qkv_ref, c_ref, cu_ref, *, tiles_per_seq):
    tm = x_ref.shape[0]
    cw = c_ref.shape[1]
    h = _rmsnorm_rows(x_ref[...], g_ref[...]).astype(BF16)
    q = jnp.dot(h, w_ref[:, :SB_WIDTH], preferred_element_type=F32) * QK_SCALE
    qkv_ref[:, :SB_WIDTH] = q.astype(BF16)
    kv = jnp.dot(h, w_ref[:, SB_WIDTH:3 * SB_WIDTH], preferred_element_type=F32)
    qkv_ref[:, SB_WIDTH:] = kv.astype(BF16)
    bcu = jnp.dot(h, w_ref[:, 3 * SB_WIDTH:], preferred_element_type=F32)
    b_gate = bcu[:, :cw]
    cu = bcu[:, cw:2 * cw] * bcu[:, 2 * cw:]

    @pl.when(pl.program_id(0) % tiles_per_seq == 0)
    def _():
        cu_ref[:SUBLANES, :] = jnp.zeros((SUBLANES, cw), F32)

    cu_ref[SUBLANES:, :] = cu
    y = cw_ref[0:1, :] * cu_ref[SUBLANES - 2:SUBLANES - 2 + tm, :]
    y = y + cw_ref[1:2, :] * cu_ref[SUBLANES - 1:SUBLANES - 1 + tm, :]
    y = y + cw_ref[2:3, :] * cu
    c_ref[...] = (b_gate * y).astype(BF16)
    cu_ref[:SUBLANES, :] = cu_ref[tm:, :]


def _in_proj_even(x, g, w, conv_w, seq):
    m, d = x.shape
    n = w.shape[1]
    cw = conv_w.shape[1]
    tm = ROW_TILE
    return pl.pallas_call(
        functools.partial(_in_proj_even_kernel, tiles_per_seq=seq // tm),
        grid=(m // tm,),
        in_specs=[
            pl.BlockSpec((tm, d), lambda i: (i, 0)),
            _whole((1, d)),
            _whole((d, n)),
            _whole((CONV_K, cw)),
        ],
        out_specs=[
            pl.BlockSpec((tm, 3 * SB_WIDTH), lambda i: (i, 0)),
            pl.BlockSpec((tm, cw), lambda i: (i, 0)),
        ],
        out_shape=[
            jax.ShapeDtypeStruct((m, 3 * SB_WIDTH), BF16),
            jax.ShapeDtypeStruct((m, cw), BF16),
        ],
        scratch_shapes=[pltpu.VMEM((tm + SUBLANES, cw), F32)],
        compiler_params=pltpu.CompilerParams(
            dimension_semantics=("arbitrary",), vmem_limit_bytes=VMEM_LIMIT),
        name="in_proj_even",
    )(x, g.reshape(1, d), w.astype(BF16), conv_w)


def _in_proj_odd_kernel(x_ref, g_ref, w_ref, o_ref, *, q_width):
    h = _rmsnorm_rows(x_ref[...], g_ref[...]).astype(BF16)
    q = jnp.dot(h, w_ref[:, :q_width], preferred_element_type=F32) * QK_SCALE
    o_ref[:, :q_width] = q.astype(BF16)
    kv = jnp.dot(h, w_ref[:, q_width:], preferred_element_type=F32)
    o_ref[:, q_width:] = kv.astype(BF16)


def _in_proj_odd(x, g, w):
    m, d = x.shape
    n = w.shape[1]
    tm = ROW_TILE
    return pl.pallas_call(
        functools.partial(_in_proj_odd_kernel, q_width=DIFF_HEADS * 2 * HEAD_DIM),
        grid=(m // tm,),
        in_specs=[
            pl.BlockSpec((tm, d), lambda i: (i, 0)),
            _whole((1, d)),
            _whole((d, n)),
        ],
        out_specs=pl.BlockSpec((tm, n), lambda i: (i, 0)),
        out_shape=jax.ShapeDtypeStruct((m, n), BF16),
        compiler_params=pltpu.CompilerParams(
            dimension_semantics=("arbitrary",), vmem_limit_bytes=VMEM_LIMIT),
        name="in_proj_odd",
    )(x, g.reshape(1, d), w.astype(BF16))


def _sb_kernel(q_ref, k_ref, v_ref, tri_ref, o_ref, carry_ref, acc_ref):
    bq = q_ref.shape[1]
    bk = bq
    i = pl.program_id(2)
    q = q_ref[0]
    lane = lax.broadcasted_iota(jnp.int32, (1, LANES), 1)
    lo_half = lane < HEAD_DIM
    zero = jnp.zeros_like(q)
    q_heads = (jnp.where(lo_half, q, zero), jnp.where(lo_half, zero, q))
    row = lax.broadcasted_iota(jnp.int32, (bq, bk), 0)
    col = lax.broadcasted_iota(jnp.int32, (bq, bk), 1)
    past = col < row
    tri = tri_ref[...]

    carry_ref[...] = jnp.zeros_like(carry_ref)
    acc_ref[...] = jnp.zeros_like(acc_ref)

    def block(j, diagonal):
        off = pl.multiple_of(j * bk, bk)
        kj = k_ref[0, pl.ds(off, bk), :]
        vj = v_ref[0, pl.ds(off, bk), :]
        for h in range(2):
            z = lax.dot_general(q_heads[h], kj, (((1,), (1,)), ((), ())),
                                preferred_element_type=F32)
            sp = jnp.maximum(z, 0.0) + jnp.log(1.0 + jnp.exp(-jnp.abs(z)))
            if diagonal:
                sp = jnp.where(past, sp, 0.0)
            hi = sp.astype(BF16)
            lo = (sp - hi.astype(F32)).astype(BF16)
            cs = jnp.dot(jnp.concatenate([hi, lo], axis=1), tri, preferred_element_type=F32)
            w = jnp.exp(z - cs[:, :bk] - carry_ref[h])
            if diagonal:
                w = jnp.where(past, w, 0.0)
            acc_ref[h] += jnp.dot(w.astype(BF16), vj, preferred_element_type=F32)
            carry_ref[h] += cs[:, bk:]

    block(i, True)

    def cond(state):
        j, go = state
        return jnp.logical_and(j >= 0, go)

    def body(state):
        j, _ = state
        block(j, False)
        least = jnp.minimum(jnp.min(carry_ref[0]), jnp.min(carry_ref[1]))
        return j - 1, least < SB_SKIP_LOG_MASS

    lax.while_loop(cond, body, (i - 1, jnp.bool_(True)))
    o_ref[0] = jnp.where(lo_half, acc_ref[0], acc_ref[1]).astype(o_ref.dtype)


def _suffix_sum_matrix(bk):
    j = lax.broadcasted_iota(jnp.int32, (2 * bk, bk + LANES), 0) % bk
    s = lax.broadcasted_iota(jnp.int32, (2 * bk, bk + LANES), 1)
    return jnp.logical_or(s >= bk, j >= s).astype(BF16)


def _sb_attention(qkv):
    b, s, _ = qkv.shape
    bq = SB_BLOCK
    pairs = SB_WIDTH // LANES
    tri = _suffix_sum_matrix(bq)
    return pl.pallas_call(
        _sb_kernel,
        grid=(b, pairs, s // bq),
        in_specs=[
            pl.BlockSpec((1, bq, LANES), lambda bi, p, i: (bi, i, p)),
            pl.BlockSpec((1, s, LANES), lambda bi, p, i: (bi, 0, pairs + p)),
            pl.BlockSpec((1, s, LANES), lambda bi, p, i: (bi, 0, 2 * pairs + p)),
            _whole(tri.shape),
        ],
        out_specs=pl.BlockSpec((1, bq, LANES), lambda bi, p, i: (bi, i, p)),
        out_shape=jax.ShapeDtypeStruct((b, s, SB_WIDTH), BF16),
        scratch_shapes=[pltpu.VMEM((2, bq, LANES), F32), pltpu.VMEM((2, bq, LANES), F32)],
        compiler_params=pltpu.CompilerParams(
            dimension_semantics=("arbitrary", "arbitrary", "arbitrary"),
            vmem_limit_bytes=VMEM_LIMIT),
        name="stickbreak_attn",
    )(qkv, qkv, qkv, tri)


def _diff_kernel(lam_ref, g_ref, q_ref, k_ref, v_ref, o_ref, m_ref, l_ref, acc_ref, *, lambda_init):
    bq = q_ref.shape[1]
    bk = bq
    i = pl.program_id(2)
    q = q_ref[0]
    lane = lax.broadcasted_iota(jnp.int32, (1, LANES), 1)
    lo_half = lane < HEAD_DIM
    zero = jnp.zeros_like(q)
    q_maps = (jnp.where(lo_half, q, zero), jnp.where(lo_half, zero, q))

    m_ref[...] = jnp.full_like(m_ref, -jnp.inf)
    l_ref[...] = jnp.zeros_like(l_ref)
    acc_ref[...] = jnp.zeros_like(acc_ref)

    def block(j, diagonal):
        off = pl.multiple_of(j * bk, bk)
        kj = k_ref[0, pl.ds(off, bk), :]
        vj = v_ref[0, pl.ds(off, bk), :]
        for mp in range(2):
            s = lax.dot_general(q_maps[mp], kj, (((1,), (1,)), ((), ())),
                                preferred_element_type=F32)
            if diagonal:
                row = lax.broadcasted_iota(jnp.int32, (bq, bk), 0)
                col = lax.broadcasted_iota(jnp.int32, (bq, bk), 1)
                s = jnp.where(col <= row, s, -jnp.inf)
            m_prev = m_ref[mp][:, :1]
            m_new = jnp.maximum(m_prev, jnp.max(s, axis=-1, keepdims=True))
            alpha = jnp.exp(m_prev - m_new)
            p = jnp.exp(s - m_new)
            l_new = alpha * l_ref[mp][:, :1] + jnp.sum(p, axis=-1, keepdims=True)
            acc_ref[mp] = alpha * acc_ref[mp] + jnp.dot(p.astype(BF16), vj,
                                                        preferred_element_type=F32)
            m_ref[mp] = jnp.broadcast_to(m_new, (bq, LANES))
            l_ref[mp] = jnp.broadcast_to(l_new, (bq, LANES))

    def body(j, c):
        block(j, False)
        return c

    lax.fori_loop(0, i, body, 0)
    block(i, True)

    lam_vecs = lam_ref[...]
    lam = (jnp.exp(jnp.sum(lam_vecs[0:1] * lam_vecs[1:2], axis=-1, keepdims=True))
           - jnp.exp(jnp.sum(lam_vecs[2:3] * lam_vecs[3:4], axis=-1, keepdims=True))
           + lambda_init)
    o = acc_ref[0] / l_ref[0] - lam * (acc_ref[1] / l_ref[1])
    o = _rmsnorm_rows(o, g_ref[...]) * (1.0 - lambda_init)
    o_ref[0] = o.astype(o_ref.dtype)


def _diff_attention(proj, lam_vecs, subln_g, lambda_init):
    b, s, _ = proj.shape
    bq = DIFF_BLOCK
    heads = DIFF_HEADS
    return pl.pallas_call(
        functools.partial(_diff_kernel, lambda_init=lambda_init),
        grid=(b, heads, s // bq),
        in_specs=[
            _whole(lam_vecs.shape),
            _whole((1, DIFF_V_DIM)),
            pl.BlockSpec((1, bq, LANES), lambda bi, h, i: (bi, i, h)),
            pl.BlockSpec((1, s, LANES), lambda bi, h, i: (bi, 0, heads + h)),
            pl.BlockSpec((1, s, LANES), lambda bi, h, i: (bi, 0, 2 * heads + h)),
        ],
        out_specs=pl.BlockSpec((1, bq, LANES), lambda bi, h, i: (bi, i, h)),
        out_shape=jax.ShapeDtypeStruct((b, s, heads * DIFF_V_DIM), BF16),
        scratch_shapes=[pltpu.VMEM((2, bq, LANES), F32)] * 3,
        compiler_params=pltpu.CompilerParams(
            dimension_semantics=("arbitrary", "arbitrary", "arbitrary"),
            vmem_limit_bytes=VMEM_LIMIT),
        name="diff_attn",
    )(lam_vecs, subln_g.reshape(1, DIFF_V_DIM), proj, proj, proj)


def _post_kernel(*refs, n_mix, final_norm):
    x_ref = refs[0]
    mix_refs = refs[1:1 + n_mix]
    wo_ref, g_ref, wu_ref, wd_ref = refs[1 + n_mix:5 + n_mix]
    gf_ref = refs[5 + n_mix] if final_norm else None
    o_ref, h_ref = refs[-2:]

    o_ref[...] = x_ref[...]
    off = 0
    for mref in mix_refs:
        width = mref.shape[1]
        o_ref[...] += jnp.dot(mref[...], wo_ref[off:off + width, :], preferred_element_type=F32)
        off += width
    h_ref[...] = _rmsnorm_rows(o_ref[...], g_ref[...]).astype(BF16)
    d_ff = wu_ref.shape[1]
    for c in range(0, d_ff, FF_CHUNK):
        up = jnp.dot(h_ref[...], wu_ref[:, c:c + FF_CHUNK], preferred_element_type=F32)
        act = jnp.square(jnp.maximum(up, 0.0)).astype(BF16)
        o_ref[...] += jnp.dot(act, wd_ref[c:c + FF_CHUNK, :], preferred_element_type=F32)
    if final_norm:
        o_ref[...] = _rmsnorm_rows(o_ref[...], gf_ref[...])


def _post(x, mixes, w_out, g_mlp, w_up, w_down, g_final):
    m, d = x.shape
    d_ff = w_up.shape[1]
    tm = ROW_TILE
    final_norm = g_final is not None
    in_specs = [pl.BlockSpec((tm, d), lambda i: (i, 0))]
    in_specs += [pl.BlockSpec((tm, mx.shape[1]), lambda i: (i, 0)) for mx in mixes]
    in_specs += [_whole(w_out.shape), _whole((1, d)), _whole((d, d_ff)), _whole((d_ff, d))]
    args = [x, *mixes, w_out.astype(BF16), g_mlp.reshape(1, d), w_up.astype(BF16),
            w_down.astype(BF16)]
    if final_norm:
        in_specs.append(_whole((1, d)))
        args.append(g_final.reshape(1, d))
    return pl.pallas_call(
        functools.partial(_post_kernel, n_mix=len(mixes), final_norm=final_norm),
        grid=(m // tm,),
        in_specs=in_specs,
        out_specs=pl.BlockSpec((tm, d), lambda i: (i, 0)),
        out_shape=jax.ShapeDtypeStruct((m, d), F32),
        scratch_shapes=[pltpu.VMEM((tm, d), BF16)],
        compiler_params=pltpu.CompilerParams(
            dimension_semantics=("arbitrary",), vmem_limit_bytes=VMEM_LIMIT),
        name="out_proj_mlp",
    )(*args)


def kernel(x, norm_mix, norm_mlp, norm_final, w_in_even, conv_w, w_out_even, w_in_odd,
           lam_q1, lam_k1, lam_q2, lam_k2, subln_g, w_out_odd, w_up, w_down):
    b, s, d = x.shape
    m = b * s
    depth = norm_mix.shape[0]
    xf = x.reshape(m, d)
    for layer in range(depth):
        i = layer // 2
        g_final = norm_final if layer == depth - 1 else None
        if layer % 2 == 0:
            qkv, c_out = _in_proj_even(xf, norm_mix[layer], w_in_even[i], conv_w[i], s)
            a_out = _sb_attention(qkv.reshape(b, s, 3 * SB_WIDTH))
            mixes = [a_out.reshape(m, SB_WIDTH), c_out]
            w_out = w_out_even[i]
        else:
            proj = _in_proj_odd(xf, norm_mix[layer], w_in_odd[i])
            lambda_init = 0.8 - 0.6 * math.exp(-0.3 * layer)
            lam_vecs = jnp.stack([lam_q1[i], lam_k1[i], lam_q2[i], lam_k2[i]])
            o = _diff_attention(proj.reshape(b, s, proj.shape[1]), lam_vecs, subln_g[i],
                                lambda_init)
            mixes = [o.reshape(m, DIFF_HEADS * DIFF_V_DIM)]
            w_out = w_out_odd[i]
        xf = _post(xf, mixes, w_out, norm_mlp[layer], w_up[layer], w_down[layer], g_final)
    return xf.reshape(b, s, d)
```

```python
import functools
import math

import jax
import jax.numpy as jnp
from jax import lax
from jax.experimental import pallas as pl
from jax.experimental.pallas import tpu as pltpu

F32 = jnp.float32
BF16 = jnp.bfloat16

NORM_EPS = 1e-6
HEAD_DIM = 64
SB_HEADS = 8
SB_WIDTH = SB_HEADS * HEAD_DIM
CONV_K = 3
DIFF_HEADS = 8
DIFF_V_DIM = 2 * HEAD_DIM
QK_SCALE = HEAD_DIM ** -0.5

LANES = 128
SUBLANES = 8
VMEM_LIMIT = 48 * 1024 * 1024

ROW_TILE = 512
FF_CHUNK = 1024
SB_BLOCK = 128
DIFF_BLOCK = 256

SB_SKIP_LOG_MASS = 50.0

MASK_VALUE = -1e30


def _rmsnorm_rows(x, g):
    ms = jnp.mean(x * x, axis=-1, keepdims=True)
    return x * lax.rsqrt(ms + NORM_EPS) * g


def _whole(shape):
    return pl.BlockSpec(shape, lambda *_: (0,) * len(shape), pipeline_mode=pl.Buffered(1))


def _in_proj_even_kernel(x_ref, g_ref, w_ref, cw_ref, qkv_ref, c_ref, cu_ref, *, tiles_per_seq):
    tm = x_ref.shape[0]
    cw = c_ref.shape[1]
    h = _rmsnorm_rows(x_ref[...], g_ref[...]).astype(BF16)
    q = jnp.dot(h, w_ref[:, :SB_WIDTH], preferred_element_type=F32) * QK_SCALE
    qkv_ref[:, :SB_WIDTH] = q.astype(BF16)
    kv = jnp.dot(h, w_ref[:, SB_WIDTH:3 * SB_WIDTH], preferred_element_type=F32)
    qkv_ref[:, SB_WIDTH:] = kv.astype(BF16)
    bcu = jnp.dot(h, w_ref[:, 3 * SB_WIDTH:], preferred_element_type=F32)
    b_gate = bcu[:, :cw]
    cu = bcu[:, cw:2 * cw] * bcu[:, 2 * cw:]

    @pl.when(pl.program_id(0) % tiles_per_seq == 0)
    def _():
        cu_ref[:SUBLANES, :] = jnp.zeros((SUBLANES, cw), F32)

    cu_ref[SUBLANES:, :] = cu
    y = cw_ref[0:1, :] * cu_ref[SUBLANES - 2:SUBLANES - 2 + tm, :]
    y = y + cw_ref[1:2, :] * cu_ref[SUBLANES - 1:SUBLANES - 1 + tm, :]
    y = y + cw_ref[2:3, :] * cu
    c_ref[...] = (b_gate * y).astype(BF16)
    cu_ref[:SUBLANES, :] = cu_ref[tm:, :]


def _in_proj_even(x, g, w, conv_w, seq):
    m, d = x.shape
    n = w.shape[1]
    cw = conv_w.shape[1]
    tm = ROW_TILE
    return pl.pallas_call(
        functools.partial(_in_proj_even_kernel, tiles_per_seq=seq // tm),
        grid=(m // tm,),
        in_specs=[
            pl.BlockSpec((tm, d), lambda i: (i, 0)),
            _whole((1, d)),
            _whole((d, n)),
            _whole((CONV_K, cw)),
        ],
        out_specs=[
            pl.BlockSpec((tm, 3 * SB_WIDTH), lambda i: (i, 0)),
            pl.BlockSpec((tm, cw), lambda i: (i, 0)),
        ],
        out_shape=[
            jax.ShapeDtypeStruct((m, 3 * SB_WIDTH), BF16),
            jax.ShapeDtypeStruct((m, cw), BF16),
        ],
        scratch_shapes=[pltpu.VMEM((tm + SUBLANES, cw), F32)],
        compiler_params=pltpu.CompilerParams(
            dimension_semantics=("arbitrary",), vmem_limit_bytes=VMEM_LIMIT),
        name="in_proj_even",
    )(x, g.reshape(1, d), w.astype(BF16), conv_w)


def _in_proj_odd_kernel(x_ref, g_ref, w_ref, o_ref, *, q_width):
    h = _rmsnorm_rows(x_ref[...], g_ref[...]).astype(BF16)
    q = jnp.dot(h, w_ref[:, :q_width], preferred_element_type=F32) * QK_SCALE
    o_ref[:, :q_width] = q.astype(BF16)
    kv = jnp.dot(h, w_ref[:, q_width:], preferred_element_type=F32)
    o_ref[:, q_width:] = kv.astype(BF16)


def _in_proj_odd(x, g, w):
    m, d = x.shape
    n = w.shape[1]
    tm = ROW_TILE
    return pl.pallas_call(
        functools.partial(_in_proj_odd_kernel, q_width=DIFF_HEADS * 2 * HEAD_DIM),
        grid=(m // tm,),
        in_specs=[
            pl.BlockSpec((tm, d), lambda i: (i, 0)),
            _whole((1, d)),
            _whole((d, n)),
        ],
        out_specs=pl.BlockSpec((tm, n), lambda i: (i, 0)),
        out_shape=jax.ShapeDtypeStruct((m, n), BF16),
        compiler_params=pltpu.CompilerParams(
            dimension_semantics=("arbitrary",), vmem_limit_bytes=VMEM_LIMIT),
        name="in_proj_odd",
    )(x, g.reshape(1, d), w.astype(BF16))


def _sb_kernel(q_ref, k_ref, v_ref, tri_ref, o_ref, carry_ref, acc_ref):
    bq = q_ref.shape[1]
    bk = bq
    i = pl.program_id(2)
    q = q_ref[0]
    lane = lax.broadcasted_iota(jnp.int32, (1, LANES), 1)
    lo_half = lane < HEAD_DIM
    zero = jnp.zeros_like(q)
    q_heads = (jnp.where(lo_half, q, zero), jnp.where(lo_half, zero, q))
    row = lax.broadcasted_iota(jnp.int32, (bq, bk), 0)
    col = lax.broadcasted_iota(jnp.int32, (bq, bk), 1)
    past = col < row
    tri = tri_ref[...]

    carry_ref[...] = jnp.zeros_like(carry_ref)
    acc_ref[...] = jnp.zeros_like(acc_ref)

    def block(j, diagonal):
        off = pl.multiple_of(j * bk, bk)
        kj = k_ref[0, pl.ds(off, bk), :]
        vj = v_ref[0, pl.ds(off, bk), :]
        for h in range(2):
            z = lax.dot_general(q_heads[h], kj, (((1,), (1,)), ((), ())),
                                preferred_element_type=F32)
            sp = jnp.maximum(z, 0.0) + jnp.log(1.0 + jnp.exp(-jnp.abs(z)))
            if diagonal:
                sp = jnp.where(past, sp, 0.0)
            hi = sp.astype(BF16)
            lo = (sp - hi.astype(F32)).astype(BF16)
            cs = jnp.dot(jnp.concatenate([hi, lo], axis=1), tri, preferred_element_type=F32)
            w = jnp.exp(z - cs[:, :bk] - carry_ref[h])
            if diagonal:
                w = jnp.where(past, w, 0.0)
            acc_ref[h] += jnp.dot(w.astype(BF16), vj, preferred_element_type=F32)
            carry_ref[h] += cs[:, bk:]

    block(i, True)

    def cond(state):
        j, go = state
        return jnp.logical_and(j >= 0, go)

    def body(state):
        j, _ = state
        block(j, False)
        least = jnp.minimum(jnp.min(carry_ref[0]), jnp.min(carry_ref[1]))
        return j - 1, least < SB_SKIP_LOG_MASS

    lax.while_loop(cond, body, (i - 1, jnp.bool_(True)))
    o_ref[0] = jnp.where(lo_half, acc_ref[0], acc_ref[1]).astype(o_ref.dtype)


def _suffix_sum_matrix(bk):
    j = lax.broadcasted_iota(jnp.int32, (2 * bk, bk + LANES), 0) % bk
    s = lax.broadcasted_iota(jnp.int32, (2 * bk, bk + LANES), 1)
    return jnp.logical_or(s >= bk, j >= s).astype(BF16)


def _sb_attention(qkv):
    b, s, _ = qkv.shape
    bq = SB_BLOCK
    pairs = SB_WIDTH // LANES
    tri = _suffix_sum_matrix(bq)
    return pl.pallas_call(
        _sb_kernel,
        grid=(b, pairs, s // bq),
        in_specs=[
            pl.BlockSpec((1, bq, LANES), lambda bi, p, i: (bi, i, p)),
            pl.BlockSpec((1, s, LANES), lambda bi, p, i: (bi, 0, pairs + p)),
            pl.BlockSpec((1, s, LANES), lambda bi, p, i: (bi, 0, 2 * pairs + p)),
            _whole(tri.shape),
        ],
        out_specs=pl.BlockSpec((1, bq, LANES), lambda bi, p, i: (bi, i, p)),
        out_shape=jax.ShapeDtypeStruct((b, s, SB_WIDTH), BF16),
        scratch_shapes=[pltpu.VMEM((2, bq, LANES), F32), pltpu.VMEM((2, bq, LANES), F32)],
        compiler_params=pltpu.CompilerParams(
            dimension_semantics=("arbitrary", "arbitrary", "arbitrary"),
            vmem_limit_bytes=VMEM_LIMIT),
        name="stickbreak_attn",
    )(qkv, qkv, qkv, tri)


def _diff_kernel(lam_ref, g_ref, q_ref, k_ref, v_ref, o_ref, vt_ref, qt_ref, s_ref, p_ref,
                 m_ref, l_ref, acc_ref, *, lambda_init):
    bq = q_ref.shape[1]
    bk = bq
    i = pl.program_id(2)
    n_blocks = vt_ref.shape[0]

    @pl.when(i == 0)
    def _():
        def tr(j, c):
            off = pl.multiple_of(j * bk, bk)
            vt_ref[j] = v_ref[0, pl.ds(off, bk), :].astype(F32).T.astype(BF16)
            return c
        lax.fori_loop(0, n_blocks, tr, 0)

    q = q_ref[0].astype(F32)
    lane = lax.broadcasted_iota(jnp.int32, (1, LANES), 1)
    lo_half = lane < HEAD_DIM
    qt_ref[0] = jnp.where(lo_half, q, 0.0).T.astype(BF16)
    qt_ref[1] = jnp.where(lo_half, 0.0, q).T.astype(BF16)

    m_ref[...] = jnp.full_like(m_ref, MASK_VALUE)
    l_ref[...] = jnp.zeros_like(l_ref)
    acc_ref[...] = jnp.zeros_like(acc_ref)

    def scores(j, slot):
        off = pl.multiple_of(j * bk, bk)
        kj = k_ref[0, pl.ds(off, bk), :]
        for mp in range(2):
            s_ref[slot, mp] = jnp.dot(kj, qt_ref[mp], preferred_element_type=F32)

    def softmax(slot, mp, diagonal):
        st = s_ref[slot, mp]
        if diagonal:
            key = lax.broadcasted_iota(jnp.int32, (bk, bq), 0)
            qry = lax.broadcasted_iota(jnp.int32, (bk, bq), 1)
            st = jnp.where(key <= qry, st, MASK_VALUE)
        m_prev = m_ref[mp]
        m_new = jnp.maximum(m_prev, jnp.max(st, axis=0, keepdims=True))
        alpha = jnp.exp(m_prev - m_new)
        pt = jnp.exp(st - m_new)
        l_ref[mp] = alpha * l_ref[mp] + jnp.sum(pt, axis=0, keepdims=True)
        m_ref[mp] = m_new
        return pt.astype(BF16), alpha

    def step(j, a):
        scores(j + 1, 1 - a)
        vt_prev = vt_ref[jnp.maximum(j - 1, 0)]
        for mp in range(2):
            pv = jnp.dot(vt_prev, p_ref[1 - a, mp], preferred_element_type=F32)
            pt, alpha = softmax(a, mp, False)
            p_ref[a, mp] = pt
            acc_ref[mp] = alpha * (acc_ref[mp] + pv)

    def last(a):
        vt_prev = vt_ref[jnp.maximum(i - 1, 0)]
        vt_cur = vt_ref[i]
        for mp in range(2):
            pv = jnp.dot(vt_prev, p_ref[1 - a, mp], preferred_element_type=F32)
            pt, alpha = softmax(a, mp, True)
            acc_ref[mp] = (alpha * (acc_ref[mp] + pv)
                           + jnp.dot(vt_cur, pt, preferred_element_type=F32))

    scores(0, 0)
    p_ref[1] = jnp.zeros_like(p_ref[1])

    def pair(t, c):
        step(2 * t, 0)
        step(2 * t + 1, 1)
        return c

    lax.fori_loop(0, i // 2, pair, 0)
    odd = i % 2 == 1

    @pl.when(odd)
    def _():
        step(i - 1, 0)
        last(1)

    @pl.when(jnp.logical_not(odd))
    def _():
        last(0)

    lam_vecs = lam_ref[...]
    lam = (jnp.exp(jnp.sum(lam_vecs[0:1] * lam_vecs[1:2], axis=-1, keepdims=True))
           - jnp.exp(jnp.sum(lam_vecs[2:3] * lam_vecs[3:4], axis=-1, keepdims=True))
           + lambda_init)
    ot = acc_ref[0] / l_ref[0] - lam * (acc_ref[1] / l_ref[1])
    o = _rmsnorm_rows(ot.T, g_ref[...]) * (1.0 - lambda_init)
    o_ref[0] = o.astype(o_ref.dtype)


def _diff_attention(proj, lam_vecs, subln_g, lambda_init):
    b, s, _ = proj.shape
    bq = DIFF_BLOCK
    heads = DIFF_HEADS
    return pl.pallas_call(
        functools.partial(_diff_kernel, lambda_init=lambda_init),
        grid=(b, heads, s // bq),
        in_specs=[
            _whole(lam_vecs.shape),
            _whole((1, DIFF_V_DIM)),
            pl.BlockSpec((1, bq, LANES), lambda bi, h, i: (bi, i, h)),
            pl.BlockSpec((1, s, LANES), lambda bi, h, i: (bi, 0, heads + h)),
            pl.BlockSpec((1, s, LANES), lambda bi, h, i: (bi, 0, 2 * heads + h)),
        ],
        out_specs=pl.BlockSpec((1, bq, LANES), lambda bi, h, i: (bi, i, h)),
        out_shape=jax.ShapeDtypeStruct((b, s, heads * DIFF_V_DIM), BF16),
        scratch_shapes=[
            pltpu.VMEM((s // bq, DIFF_V_DIM, bq), BF16),
            pltpu.VMEM((2, LANES, bq), BF16),
            pltpu.VMEM((2, 2, bq, bq), F32),
            pltpu.VMEM((2, 2, bq, bq), BF16),
            pltpu.VMEM((2, 1, bq), F32),
            pltpu.VMEM((2, 1, bq), F32),
            pltpu.VMEM((2, DIFF_V_DIM, bq), F32),
        ],
        compiler_params=pltpu.CompilerParams(
            dimension_semantics=("arbitrary", "arbitrary", "arbitrary"),
            vmem_limit_bytes=VMEM_LIMIT),
        name="diff_attn",
    )(lam_vecs, subln_g.reshape(1, DIFF_V_DIM), proj, proj, proj)


def _post_kernel(*refs, n_mix, final_norm):
    x_ref = refs[0]
    mix_refs = refs[1:1 + n_mix]
    wo_ref, g_ref, wu_ref, wd_ref = refs[1 + n_mix:5 + n_mix]
    gf_ref = refs[5 + n_mix] if final_norm else None
    o_ref, h_ref = refs[-2:]

    o_ref[...] = x_ref[...]
    off = 0
    for mref in mix_refs:
        width = mref.shape[1]
        o_ref[...] += jnp.dot(mref[...], wo_ref[off:off + width, :], preferred_element_type=F32)
        off += width
    h_ref[...] = _rmsnorm_rows(o_ref[...], g_ref[...]).astype(BF16)
    d_ff = wu_ref.shape[1]
    for c in range(0, d_ff, FF_CHUNK):
        up = jnp.dot(h_ref[...], wu_ref[:, c:c + FF_CHUNK], preferred_element_type=F32)
        act = jnp.square(jnp.maximum(up, 0.0)).astype(BF16)
        o_ref[...] += jnp.dot(act, wd_ref[c:c + FF_CHUNK, :], preferred_element_type=F32)
    if final_norm:
        o_ref[...] = _rmsnorm_rows(o_ref[...], gf_ref[...])


def _post(x, mixes, w_out, g_mlp, w_up, w_down, g_final):
    m, d = x.shape
    d_ff = w_up.shape[1]
    tm = ROW_TILE
    final_norm = g_final is not None
    in_specs = [pl.BlockSpec((tm, d), lambda i: (i, 0))]
    in_specs += [pl.BlockSpec((tm, mx.shape[1]), lambda i: (i, 0)) for mx in mixes]
    in_specs += [_whole(w_out.shape), _whole((1, d)), _whole((d, d_ff)), _whole((d_ff, d))]
    args = [x, *mixes, w_out.astype(BF16), g_mlp.reshape(1, d), w_up.astype(BF16),
            w_down.astype(BF16)]
    if final_norm:
        in_specs.append(_whole((1, d)))
        args.append(g_final.reshape(1, d))
    return pl.pallas_call(
        functools.partial(_post_kernel, n_mix=len(mixes), final_norm=final_norm),
        grid=(m // tm,),
        in_specs=in_specs,
        out_specs=pl.BlockSpec((tm, d), lambda i: (i, 0)),
        out_shape=jax.ShapeDtypeStruct((m, d), F32),
        scratch_shapes=[pltpu.VMEM((tm, d), BF16)],
        compiler_params=pltpu.CompilerParams(
            dimension_semantics=("arbitrary",), vmem_limit_bytes=VMEM_LIMIT),
        name="out_proj_mlp",
    )(*args)


def kernel(x, norm_mix, norm_mlp, norm_final, w_in_even, conv_w, w_out_even, w_in_odd,
           lam_q1, lam_k1, lam_q2, lam_k2, subln_g, w_out_odd, w_up, w_down):
    b, s, d = x.shape
    m = b * s
    depth = norm_mix.shape[0]
    xf = x.reshape(m, d)
    for layer in range(depth):
        i = layer // 2
        g_final = norm_final if layer == depth - 1 else None
        if layer % 2 == 0:
            qkv, c_out = _in_proj_even(xf, norm_mix[layer], w_in_even[i], conv_w[i], s)
            a_out = _sb_attention(qkv.reshape(b, s, 3 * SB_WIDTH))
            mixes = [a_out.reshape(m, SB_WIDTH), c_out]
            w_out = w_out_even[i]
        else:
            proj = _in_proj_odd(xf, norm_mix[layer], w_in_odd[i])
            lambda_init = 0.8 - 0.6 * math.exp(-0.3 * layer)
            lam_vecs = jnp.stack([lam_q1[i], lam_k1[i], lam_q2[i], lam_k2[i]])
            o = _diff_attention(proj.reshape(b, s, proj.shape[1]), lam_vecs, subln_g[i],
                                lambda_init)
            mixes = [o.reshape(m, DIFF_HEADS * DIFF_V_DIM)]
            w_out = w_out_odd[i]
        xf = _post(xf, mixes, w_out, norm_mlp[layer], w_up[layer], w_down[layer], g_final)
    return xf.reshape(b, s, d)
```

```python
import functools
import math

import jax
import jax.numpy as jnp
from jax import lax
from jax.experimental import pallas as pl
from jax.experimental.pallas import tpu as pltpu

F32 = jnp.float32
BF16 = jnp.bfloat16

NORM_EPS = 1e-6
HEAD_DIM = 64
SB_HEADS = 8
SB_WIDTH = SB_HEADS * HEAD_DIM
CONV_K = 3
DIFF_HEADS = 8
DIFF_V_DIM = 2 * HEAD_DIM
QK_SCALE = HEAD_DIM ** -0.5
LOG2E = math.log2(math.e)
ONES_ROWS = 16

LANES = 128
SUBLANES = 8
VMEM_LIMIT = 48 * 1024 * 1024

ROW_TILE = 512
FF_CHUNK = 1024
SB_BLOCK = 128
DIFF_BLOCK = 256
DIFF_HEADS_PER_STEP = 1

SB_SKIP_LOG_MASS = 50.0

MASK_VALUE = -1e30


def _rmsnorm_rows(x, g):
    ms = jnp.mean(x * x, axis=-1, keepdims=True)
    return x * lax.rsqrt(ms + NORM_EPS) * g


def _whole(shape):
    return pl.BlockSpec(shape, lambda *_: (0,) * len(shape), pipeline_mode=pl.Buffered(1))


def _in_proj_even_kernel(x_ref, g_ref, w_ref, cw_ref, qkv_ref, c_ref, cu_ref, *, tiles_per_seq):
    tm = x_ref.shape[0]
    cw = c_ref.shape[1]
    h = _rmsnorm_rows(x_ref[...], g_ref[...]).astype(BF16)
    q = jnp.dot(h, w_ref[:, :SB_WIDTH], preferred_element_type=F32) * QK_SCALE
    qkv_ref[:, :SB_WIDTH] = q.astype(BF16)
    kv = jnp.dot(h, w_ref[:, SB_WIDTH:3 * SB_WIDTH], preferred_element_type=F32)
    qkv_ref[:, SB_WIDTH:] = kv.astype(BF16)
    bcu = jnp.dot(h, w_ref[:, 3 * SB_WIDTH:], preferred_element_type=F32)
    b_gate = bcu[:, :cw]
    cu = bcu[:, cw:2 * cw] * bcu[:, 2 * cw:]

    @pl.when(pl.program_id(0) % tiles_per_seq == 0)
    def _():
        cu_ref[:SUBLANES, :] = jnp.zeros((SUBLANES, cw), F32)

    cu_ref[SUBLANES:, :] = cu
    y = cw_ref[0:1, :] * cu_ref[SUBLANES - 2:SUBLANES - 2 + tm, :]
    y = y + cw_ref[1:2, :] * cu_ref[SUBLANES - 1:SUBLANES - 1 + tm, :]
    y = y + cw_ref[2:3, :] * cu
    c_ref[...] = (b_gate * y).astype(BF16)
    cu_ref[:SUBLANES, :] = cu_ref[tm:, :]


def _in_proj_even(x, g, w, conv_w, seq):
    m, d = x.shape
    n = w.shape[1]
    cw = conv_w.shape[1]
    tm = ROW_TILE
    return pl.pallas_call(
        functools.partial(_in_proj_even_kernel, tiles_per_seq=seq // tm),
        grid=(m // tm,),
        in_specs=[
            pl.BlockSpec((tm, d), lambda i: (i, 0)),
            _whole((1, d)),
            _whole((d, n)),
            _whole((CONV_K, cw)),
        ],
        out_specs=[
            pl.BlockSpec((tm, 3 * SB_WIDTH), lambda i: (i, 0)),
            pl.BlockSpec((tm, cw), lambda i: (i, 0)),
        ],
        out_shape=[
            jax.ShapeDtypeStruct((m, 3 * SB_WIDTH), BF16),
            jax.ShapeDtypeStruct((m, cw), BF16),
        ],
        scratch_shapes=[pltpu.VMEM((tm + SUBLANES, cw), F32)],
        compiler_params=pltpu.CompilerParams(
            dimension_semantics=("arbitrary",), vmem_limit_bytes=VMEM_LIMIT),
        name="in_proj_even",
    )(x, g.reshape(1, d), w.astype(BF16), conv_w)


def _in_proj_odd_kernel(x_ref, g_ref, w_ref, o_ref, *, q_width):
    h = _rmsnorm_rows(x_ref[...], g_ref[...]).astype(BF16)
    q = jnp.dot(h, w_ref[:, :q_width], preferred_element_type=F32) * (QK_SCALE * LOG2E)
    o_ref[:, :q_width] = q.astype(BF16)
    kv = jnp.dot(h, w_ref[:, q_width:], preferred_element_type=F32)
    o_ref[:, q_width:] = kv.astype(BF16)


def _in_proj_odd(x, g, w):
    m, d = x.shape
    n = w.shape[1]
    tm = ROW_TILE
    return pl.pallas_call(
        functools.partial(_in_proj_odd_kernel, q_width=DIFF_HEADS * 2 * HEAD_DIM),
        grid=(m // tm,),
        in_specs=[
            pl.BlockSpec((tm, d), lambda i: (i, 0)),
            _whole((1, d)),
            _whole((d, n)),
        ],
        out_specs=pl.BlockSpec((tm, n), lambda i: (i, 0)),
        out_shape=jax.ShapeDtypeStruct((m, n), BF16),
        compiler_params=pltpu.CompilerParams(
            dimension_semantics=("arbitrary",), vmem_limit_bytes=VMEM_LIMIT),
        name="in_proj_odd",
    )(x, g.reshape(1, d), w.astype(BF16))


def _sb_kernel(q_ref, k_ref, v_ref, tri_ref, o_ref, vt_ref, qt_ref, z_ref, hl_ref, w_ref,
               carry_ref, acc_ref):
    bq = q_ref.shape[1]
    bk = bq
    i = pl.program_id(1)
    n_pairs, n_blocks = vt_ref.shape[:2]

    def pair_cols(p):
        return slice(p * LANES, (p + 1) * LANES)

    @pl.when(i == 0)
    def _():
        def tr(j, c):
            off = pl.multiple_of(j * bk, bk)
            for p in range(n_pairs):
                vj = v_ref[0, pl.ds(off, bk), pair_cols(p)]
                vt_ref[p, j] = vj.astype(F32).T.astype(BF16)
            return c
        lax.fori_loop(0, n_blocks, tr, 0)

    lane = lax.broadcasted_iota(jnp.int32, (1, LANES), 1)
    lo_half = lane < HEAD_DIM
    for p in range(n_pairs):
        q = q_ref[0, :, pair_cols(p)].astype(F32)
        qt_ref[p, :, :bq] = jnp.where(lo_half, q, 0.0).T.astype(BF16)
        qt_ref[p, :, bq:] = jnp.where(lo_half, 0.0, q).T.astype(BF16)

    tri = tri_ref[...]

    def softplus(zt):
        return jnp.maximum(zt, 0.0) + jnp.log(1.0 + jnp.exp(-jnp.abs(zt)))

    def split_bf16(sp):
        hi = sp.astype(BF16)
        return hi, (sp - hi.astype(F32)).astype(BF16)

    first = jnp.maximum(i - 1, 0)
    key = lax.broadcasted_iota(jnp.int32, (2 * bk, 2 * bq), 0)
    qry = lax.broadcasted_iota(jnp.int32, (2 * bk, 2 * bq), 1) % bq
    keep = key - qry < (i - first) * bq
    for p in range(n_pairs):
        k2 = k_ref[0, pl.ds(pl.multiple_of(first * bk, bk), 2 * bk), pair_cols(p)]
        zt = jnp.dot(k2, qt_ref[p], preferred_element_type=F32)
        zt = jnp.where(keep, zt, MASK_VALUE)
        z_ref[p] = zt
        hi, lo = split_bf16(softplus(zt))
        for half in range(2):
            rows = slice(half * bk, (half + 1) * bk)
            hl_ref[p, half] = jnp.concatenate([hi[rows], lo[rows]], axis=0)
    for p in range(n_pairs):
        cs_new = jnp.dot(tri, hl_ref[p, 1], preferred_element_type=F32)
        cs_old = jnp.dot(tri, hl_ref[p, 0], preferred_element_type=F32)
        total_new = cs_new[bk:bk + 1]
        w_ref[p, 1] = jnp.exp(z_ref[p, bk:, :] - cs_new[:bk]).astype(BF16)
        w_ref[p, 0] = jnp.exp(z_ref[p, :bk, :] - cs_old[:bk] - total_new).astype(BF16)
        carry_ref[p] = total_new + cs_old[bk:bk + 1]
    for p in range(n_pairs):
        acc_ref[p] = (jnp.dot(vt_ref[p, first], w_ref[p, 0], preferred_element_type=F32)
                      + jnp.dot(vt_ref[p, first + 1], w_ref[p, 1], preferred_element_type=F32))

    def least_carry():
        lows = [jnp.min(carry_ref[p]) for p in range(n_pairs)]
        return functools.reduce(jnp.minimum, lows)

    def cond(state):
        j, go = state
        return jnp.logical_and(j >= 0, go)

    def body(state):
        j, _ = state
        off = pl.multiple_of(j * bk, bk)
        for p in range(n_pairs):
            kj = k_ref[0, pl.ds(off, bk), pair_cols(p)]
            zt = jnp.dot(kj, qt_ref[p], preferred_element_type=F32)
            hi, lo = split_bf16(softplus(zt))
            cs = jnp.dot(tri, jnp.concatenate([hi, lo], axis=0), preferred_element_type=F32)
            wt = jnp.exp(zt - cs[:bk] - carry_ref[p]).astype(BF16)
            acc_ref[p] += jnp.dot(vt_ref[p, j], wt, preferred_element_type=F32)
            carry_ref[p] += cs[bk:bk + 1]
        return j - 1, least_carry() < SB_SKIP_LOG_MASS

    lax.while_loop(cond, body, (first - 1, least_carry() < SB_SKIP_LOG_MASS))

    feat = lax.broadcasted_iota(jnp.int32, (LANES, bq), 0)
    for p in range(n_pairs):
        res = acc_ref[p]
        ot = jnp.where(feat < HEAD_DIM, res[:, :bq], res[:, bq:])
        o_ref[0, :, pair_cols(p)] = ot.T.astype(o_ref.dtype)


def _suffix_sum_matrix(bk):
    s = lax.broadcasted_iota(jnp.int32, (bk + ONES_ROWS, 2 * bk), 0)
    j = lax.broadcasted_iota(jnp.int32, (bk + ONES_ROWS, 2 * bk), 1) % bk
    return jnp.logical_or(s >= bk, j >= s).astype(BF16)


def _sb_attention(qkv):
    b, s, _ = qkv.shape
    bq = SB_BLOCK
    pairs = SB_WIDTH // LANES
    tri = _suffix_sum_matrix(bq)
    return pl.pallas_call(
        _sb_kernel,
        grid=(b, s // bq),
        in_specs=[
            pl.BlockSpec((1, bq, SB_WIDTH), lambda bi, i: (bi, i, 0)),
            pl.BlockSpec((1, s, SB_WIDTH), lambda bi, i: (bi, 0, 1), pipeline_mode=pl.Buffered(1)),
            pl.BlockSpec((1, s, SB_WIDTH), lambda bi, i: (bi, 0, 2), pipeline_mode=pl.Buffered(1)),
            _whole(tri.shape),
        ],
        out_specs=pl.BlockSpec((1, bq, SB_WIDTH), lambda bi, i: (bi, i, 0)),
        out_shape=jax.ShapeDtypeStruct((b, s, SB_WIDTH), BF16),
        scratch_shapes=[
            pltpu.VMEM((pairs, s // bq, LANES, bq), BF16),
            pltpu.VMEM((pairs, LANES, 2 * bq), BF16),
            pltpu.VMEM((pairs, 2 * bq, 2 * bq), F32),
            pltpu.VMEM((pairs, 2, 2 * bq, 2 * bq), BF16),
            pltpu.VMEM((pairs, 2, bq, 2 * bq), BF16),
            pltpu.VMEM((pairs, 1, 2 * bq), F32),
            pltpu.VMEM((pairs, LANES, 2 * bq), F32),
        ],
        compiler_params=pltpu.CompilerParams(
            dimension_semantics=("arbitrary", "arbitrary"), vmem_limit_bytes=VMEM_LIMIT),
        name="stickbreak_attn",
    )(qkv, qkv, qkv, tri)


def _diff_kernel(lam_ref, g_ref, q_ref, k_ref, v_ref, o_ref, vt_ref, qt_ref, s_ref, p_ref,
                 m_ref, acc_ref, *, lambda_init):
    bq = q_ref.shape[1]
    bk = bq
    i = pl.program_id(2)
    n_heads, n_blocks = vt_ref.shape[:2]
    dv = DIFF_V_DIM
    chains = [(g, mp) for g in range(n_heads) for mp in range(2)]

    def head_cols(g):
        return slice(g * LANES, (g + 1) * LANES)

    @pl.when(i == 0)
    def _():
        def tr(j, c):
            off = pl.multiple_of(j * bk, bk)
            for g in range(n_heads):
                vj = v_ref[0, pl.ds(off, bk), head_cols(g)]
                vt_ref[g, j, :dv, :] = vj.astype(F32).T.astype(BF16)
                vt_ref[g, j, dv:, :] = jnp.ones((ONES_ROWS, bk), BF16)
            return c
        lax.fori_loop(0, n_blocks, tr, 0)

    lane = lax.broadcasted_iota(jnp.int32, (1, LANES), 1)
    lo_half = lane < HEAD_DIM
    for g in range(n_heads):
        q = q_ref[0, :, head_cols(g)].astype(F32)
        qt_ref[g, 0] = jnp.where(lo_half, q, 0.0).T.astype(BF16)
        qt_ref[g, 1] = jnp.where(lo_half, 0.0, q).T.astype(BF16)

    m_ref[...] = jnp.full_like(m_ref, MASK_VALUE)
    acc_ref[...] = jnp.zeros_like(acc_ref)

    def scores(j, slot):
        off = pl.multiple_of(j * bk, bk)
        for g in range(n_heads):
            kj = k_ref[0, pl.ds(off, bk), head_cols(g)]
            for mp in range(2):
                s_ref[slot, g, mp] = jnp.dot(kj, qt_ref[g, mp], preferred_element_type=F32)

    def softmax(slot, g, mp, diagonal):
        st = s_ref[slot, g, mp]
        if diagonal:
            key = lax.broadcasted_iota(jnp.int32, (bk, bq), 0)
            qry = lax.broadcasted_iota(jnp.int32, (bk, bq), 1)
            st = jnp.where(key <= qry, st, MASK_VALUE)
        m_prev = m_ref[g, mp]
        m_new = jnp.maximum(m_prev, jnp.max(st, axis=0, keepdims=True))
        alpha = jnp.exp2(m_prev - m_new)
        pt = jnp.exp2(st - m_new)
        m_ref[g, mp] = m_new
        return pt.astype(BF16), alpha

    def step(j, a):
        scores(j + 1, 1 - a)
        j_prev = jnp.maximum(j - 1, 0)
        for g, mp in chains:
            pv = jnp.dot(vt_ref[g, j_prev], p_ref[1 - a, g, mp], preferred_element_type=F32)
            pt, alpha = softmax(a, g, mp, False)
            p_ref[a, g, mp] = pt
            acc_ref[g, mp] = alpha * (acc_ref[g, mp] + pv)

    def last(a):
        j_prev = jnp.maximum(i - 1, 0)
        for g, mp in chains:
            pv = jnp.dot(vt_ref[g, j_prev], p_ref[1 - a, g, mp], preferred_element_type=F32)
            pt, alpha = softmax(a, g, mp, True)
            acc_ref[g, mp] = (alpha * (acc_ref[g, mp] + pv)
                              + jnp.dot(vt_ref[g, i], pt, preferred_element_type=F32))

    scores(0, 0)
    p_ref[1] = jnp.zeros_like(p_ref[1])

    def pair(t, c):
        step(2 * t, 0)
        step(2 * t + 1, 1)
        return c

    lax.fori_loop(0, i // 2, pair, 0)
    odd = i % 2 == 1

    @pl.when(odd)
    def _():
        step(i - 1, 0)
        last(1)

    @pl.when(jnp.logical_not(odd))
    def _():
        last(0)

    lam_vecs = lam_ref[...]
    lam = (jnp.exp(jnp.sum(lam_vecs[0:1] * lam_vecs[1:2], axis=-1, keepdims=True))
           - jnp.exp(jnp.sum(lam_vecs[2:3] * lam_vecs[3:4], axis=-1, keepdims=True))
           + lambda_init)
    for g in range(n_heads):
        ot = (acc_ref[g, 0, :dv] / acc_ref[g, 0, dv:dv + 1]
              - lam * (acc_ref[g, 1, :dv] / acc_ref[g, 1, dv:dv + 1]))
        o = _rmsnorm_rows(ot.T, g_ref[...]) * (1.0 - lambda_init)
        o_ref[0, :, head_cols(g)] = o.astype(o_ref.dtype)


def _diff_attention(proj, lam_vecs, subln_g, lambda_init):
    b, s, _ = proj.shape
    bq = DIFF_BLOCK
    hps = DIFF_HEADS_PER_STEP
    groups = DIFF_HEADS // hps
    width = hps * LANES
    return pl.pallas_call(
        functools.partial(_diff_kernel, lambda_init=lambda_init),
        grid=(b, groups, s // bq),
        in_specs=[
            _whole(lam_vecs.shape),
            _whole((1, DIFF_V_DIM)),
            pl.BlockSpec((1, bq, width), lambda bi, h, i: (bi, i, h)),
            pl.BlockSpec((1, s, width), lambda bi, h, i: (bi, 0, groups + h)),
            pl.BlockSpec((1, s, width), lambda bi, h, i: (bi, 0, 2 * groups + h)),
        ],
        out_specs=pl.BlockSpec((1, bq, width), lambda bi, h, i: (bi, i, h)),
        out_shape=jax.ShapeDtypeStruct((b, s, DIFF_HEADS * DIFF_V_DIM), BF16),
        scratch_shapes=[
            pltpu.VMEM((hps, s // bq, DIFF_V_DIM + ONES_ROWS, bq), BF16),
            pltpu.VMEM((hps, 2, LANES, bq), BF16),
            pltpu.VMEM((2, hps, 2, bq, bq), F32),
            pltpu.VMEM((2, hps, 2, bq, bq), BF16),
            pltpu.VMEM((hps, 2, 1, bq), F32),
            pltpu.VMEM((hps, 2, DIFF_V_DIM + ONES_ROWS, bq), F32),
        ],
        compiler_params=pltpu.CompilerParams(
            dimension_semantics=("arbitrary", "arbitrary", "arbitrary"),
            vmem_limit_bytes=VMEM_LIMIT),
        name="diff_attn",
    )(lam_vecs, subln_g.reshape(1, DIFF_V_DIM), proj, proj, proj)


def _post_kernel(*refs, n_mix, final_norm):
    x_ref = refs[0]
    mix_refs = refs[1:1 + n_mix]
    wo_ref, g_ref, wu_ref, wd_ref = refs[1 + n_mix:5 + n_mix]
    gf_ref = refs[5 + n_mix] if final_norm else None
    o_ref, h_ref = refs[-2:]

    o_ref[...] = x_ref[...]
    off = 0
    for mref in mix_refs:
        width = mref.shape[1]
        o_ref[...] += jnp.dot(mref[...], wo_ref[off:off + width, :], preferred_element_type=F32)
        off += width
    h_ref[...] = _rmsnorm_rows(o_ref[...], g_ref[...]).astype(BF16)
    d_ff = wu_ref.shape[1]
    for c in range(0, d_ff, FF_CHUNK):
        up = jnp.dot(h_ref[...], wu_ref[:, c:c + FF_CHUNK], preferred_element_type=F32)
        act = jnp.square(jnp.maximum(up, 0.0)).astype(BF16)
        o_ref[...] += jnp.dot(act, wd_ref[c:c + FF_CHUNK, :], preferred_element_type=F32)
    if final_norm:
        o_ref[...] = _rmsnorm_rows(o_ref[...], gf_ref[...])


def _post(x, mixes, w_out, g_mlp, w_up, w_down, g_final):
    m, d = x.shape
    d_ff = w_up.shape[1]
    tm = ROW_TILE
    final_norm = g_final is not None
    in_specs = [pl.BlockSpec((tm, d), lambda i: (i, 0))]
    in_specs += [pl.BlockSpec((tm, mx.shape[1]), lambda i: (i, 0)) for mx in mixes]
    in_specs += [_whole(w_out.shape), _whole((1, d)), _whole((d, d_ff)), _whole((d_ff, d))]
    args = [x, *mixes, w_out.astype(BF16), g_mlp.reshape(1, d), w_up.astype(BF16),
            w_down.astype(BF16)]
    if final_norm:
        in_specs.append(_whole((1, d)))
        args.append(g_final.reshape(1, d))
    return pl.pallas_call(
        functools.partial(_post_kernel, n_mix=len(mixes), final_norm=final_norm),
        grid=(m // tm,),
        in_specs=in_specs,
        out_specs=pl.BlockSpec((tm, d), lambda i: (i, 0)),
        out_shape=jax.ShapeDtypeStruct((m, d), F32),
        scratch_shapes=[pltpu.VMEM((tm, d), BF16)],
        compiler_params=pltpu.CompilerParams(
            dimension_semantics=("arbitrary",), vmem_limit_bytes=VMEM_LIMIT),
        name="out_proj_mlp",
    )(*args)


def kernel(x, norm_mix, norm_mlp, norm_final, w_in_even, conv_w, w_out_even, w_in_odd,
           lam_q1, lam_k1, lam_q2, lam_k2, subln_g, w_out_odd, w_up, w_down):
    b, s, d = x.shape
    m = b * s
    depth = norm_mix.shape[0]
    xf = x.reshape(m, d)
    for layer in range(depth):
        i = layer // 2
        g_final = norm_final if layer == depth - 1 else None
        if layer % 2 == 0:
            qkv, c_out = _in_proj_even(xf, norm_mix[layer], w_in_even[i], conv_w[i], s)
            a_out = _sb_attention(qkv.reshape(b, s, 3 * SB_WIDTH))
            mixes = [a_out.reshape(m, SB_WIDTH), c_out]
            w_out = w_out_even[i]
        else:
            proj = _in_proj_odd(xf, norm_mix[layer], w_in_odd[i])
            lambda_init = 0.8 - 0.6 * math.exp(-0.3 * layer)
            lam_vecs = jnp.stack([lam_q1[i], lam_k1[i], lam_q2[i], lam_k2[i]])
            o = _diff_attention(proj.reshape(b, s, proj.shape[1]), lam_vecs, subln_g[i],
                                lambda_init)
            mixes = [o.reshape(m, DIFF_HEADS * DIFF_V_DIM)]
            w_out = w_out_odd[i]
        xf = _post(xf, mixes, w_out, norm_mlp[layer], w_up[layer], w_down[layer], g_final)
    return xf.reshape(b, s, d)
```

```python
import functools
import math

import jax
import jax.numpy as jnp
from jax import lax
from jax.experimental import pallas as pl
from jax.experimental.pallas import tpu as pltpu

F32 = jnp.float32
BF16 = jnp.bfloat16

NORM_EPS = 1e-6
HEAD_DIM = 64
SB_HEADS = 8
SB_WIDTH = SB_HEADS * HEAD_DIM
CONV_K = 3
DIFF_HEADS = 8
DIFF_V_DIM = 2 * HEAD_DIM
QK_SCALE = HEAD_DIM ** -0.5
LOG2E = math.log2(math.e)
ONES_ROWS = 16

LANES = 128
SUBLANES = 8
VMEM_LIMIT = 48 * 1024 * 1024

ROW_TILE = 512
FF_CHUNK = 1024
SB_BLOCK = 128
DIFF_BLOCK = 256
DIFF_UNROLL = 16
TILE_LOOP_UNROLL = 4

SB_SKIP_LOG_MASS = 50.0

MASK_VALUE = -1e30


def _rmsnorm_rows(x, g):
    ms = jnp.mean(x * x, axis=-1, keepdims=True)
    return x * lax.rsqrt(ms + NORM_EPS) * g


def _whole(shape):
    return pl.BlockSpec(shape, lambda *_: (0,) * len(shape), pipeline_mode=pl.Buffered(1))


def _in_proj_even_kernel(x_ref, g_ref, w_ref, cw_ref, qkv_ref, c_ref, cu_ref, *, tiles_per_seq):
    tm = x_ref.shape[0]
    cw = c_ref.shape[1]
    h = _rmsnorm_rows(x_ref[...], g_ref[...]).astype(BF16)
    q = jnp.dot(h, w_ref[:, :SB_WIDTH], preferred_element_type=F32) * QK_SCALE
    qkv_ref[:, :SB_WIDTH] = q.astype(BF16)
    kv = jnp.dot(h, w_ref[:, SB_WIDTH:3 * SB_WIDTH], preferred_element_type=F32)
    qkv_ref[:, SB_WIDTH:] = kv.astype(BF16)
    bcu = jnp.dot(h, w_ref[:, 3 * SB_WIDTH:], preferred_element_type=F32)
    b_gate = bcu[:, :cw]
    cu = bcu[:, cw:2 * cw] * bcu[:, 2 * cw:]

    @pl.when(pl.program_id(0) % tiles_per_seq == 0)
    def _():
        cu_ref[:SUBLANES, :] = jnp.zeros((SUBLANES, cw), F32)

    cu_ref[SUBLANES:, :] = cu
    y = cw_ref[0:1, :] * cu_ref[SUBLANES - 2:SUBLANES - 2 + tm, :]
    y = y + cw_ref[1:2, :] * cu_ref[SUBLANES - 1:SUBLANES - 1 + tm, :]
    y = y + cw_ref[2:3, :] * cu
    c_ref[...] = (b_gate * y).astype(BF16)
    cu_ref[:SUBLANES, :] = cu_ref[tm:, :]


def _in_proj_even(x, g, w, conv_w, seq):
    m, d = x.shape
    n = w.shape[1]
    cw = conv_w.shape[1]
    tm = ROW_TILE
    return pl.pallas_call(
        functools.partial(_in_proj_even_kernel, tiles_per_seq=seq // tm),
        grid=(m // tm,),
        in_specs=[
            pl.BlockSpec((tm, d), lambda i: (i, 0)),
            _whole((1, d)),
            _whole((d, n)),
            _whole((CONV_K, cw)),
        ],
        out_specs=[
            pl.BlockSpec((tm, 3 * SB_WIDTH), lambda i: (i, 0)),
            pl.BlockSpec((tm, cw), lambda i: (i, 0)),
        ],
        out_shape=[
            jax.ShapeDtypeStruct((m, 3 * SB_WIDTH), BF16),
            jax.ShapeDtypeStruct((m, cw), BF16),
        ],
        scratch_shapes=[pltpu.VMEM((tm + SUBLANES, cw), F32)],
        compiler_params=pltpu.CompilerParams(
            dimension_semantics=("arbitrary",), vmem_limit_bytes=VMEM_LIMIT),
        name="in_proj_even",
    )(x, g.reshape(1, d), w.astype(BF16), conv_w)


def _in_proj_odd_kernel(x_ref, g_ref, w_ref, o_ref, *, q_width):
    h = _rmsnorm_rows(x_ref[...], g_ref[...]).astype(BF16)
    q = jnp.dot(h, w_ref[:, :q_width], preferred_element_type=F32) * (QK_SCALE * LOG2E)
    o_ref[:, :q_width] = q.astype(BF16)
    kv = jnp.dot(h, w_ref[:, q_width:], preferred_element_type=F32)
    o_ref[:, q_width:] = kv.astype(BF16)


def _in_proj_odd(x, g, w):
    m, d = x.shape
    n = w.shape[1]
    tm = ROW_TILE
    return pl.pallas_call(
        functools.partial(_in_proj_odd_kernel, q_width=DIFF_HEADS * 2 * HEAD_DIM),
        grid=(m // tm,),
        in_specs=[
            pl.BlockSpec((tm, d), lambda i: (i, 0)),
            _whole((1, d)),
            _whole((d, n)),
        ],
        out_specs=pl.BlockSpec((tm, n), lambda i: (i, 0)),
        out_shape=jax.ShapeDtypeStruct((m, n), BF16),
        compiler_params=pltpu.CompilerParams(
            dimension_semantics=("arbitrary",), vmem_limit_bytes=VMEM_LIMIT),
        name="in_proj_odd",
    )(x, g.reshape(1, d), w.astype(BF16))


def _sb_kernel(q_ref, k_ref, v_ref, tri_ref, o_ref, vt_ref, qt_ref, z_ref, hl_ref, w_ref,
               carry_ref, acc_ref):
    bq = q_ref.shape[1]
    bk = bq
    i = pl.program_id(1)
    n_pairs, n_blocks = vt_ref.shape[:2]

    def pair_cols(p):
        return slice(p * LANES, (p + 1) * LANES)

    @pl.when(i == 0)
    def _():
        def tr(j, c):
            off = pl.multiple_of(j * bk, bk)
            for p in range(n_pairs):
                vj = v_ref[0, pl.ds(off, bk), pair_cols(p)]
                vt_ref[p, j] = vj.astype(F32).T.astype(BF16)
            return c
        lax.fori_loop(0, n_blocks, tr, 0)

    lane = lax.broadcasted_iota(jnp.int32, (1, LANES), 1)
    lo_half = lane < HEAD_DIM
    for p in range(n_pairs):
        q = q_ref[0, :, pair_cols(p)].astype(F32)
        qt_ref[p, :, :bq] = jnp.where(lo_half, q, 0.0).T.astype(BF16)
        qt_ref[p, :, bq:] = jnp.where(lo_half, 0.0, q).T.astype(BF16)

    tri = tri_ref[...]

    def softplus(zt):
        return jnp.maximum(zt, 0.0) + jnp.log(1.0 + jnp.exp(-jnp.abs(zt)))

    def split_bf16(sp):
        hi = sp.astype(BF16)
        return hi, (sp - hi.astype(F32)).astype(BF16)

    first = jnp.maximum(i - 1, 0)
    key = lax.broadcasted_iota(jnp.int32, (2 * bk, 2 * bq), 0)
    qry = lax.broadcasted_iota(jnp.int32, (2 * bk, 2 * bq), 1) % bq
    keep = key - qry < (i - first) * bq
    for p in range(n_pairs):
        k2 = k_ref[0, pl.ds(pl.multiple_of(first * bk, bk), 2 * bk), pair_cols(p)]
        zt = jnp.dot(k2, qt_ref[p], preferred_element_type=F32)
        zt = jnp.where(keep, zt, MASK_VALUE)
        z_ref[p] = zt
        hi, lo = split_bf16(softplus(zt))
        for half in range(2):
            rows = slice(half * bk, (half + 1) * bk)
            hl_ref[p, half] = jnp.concatenate([hi[rows], lo[rows]], axis=0)
    for p in range(n_pairs):
        cs_new = jnp.dot(tri, hl_ref[p, 1], preferred_element_type=F32)
        cs_old = jnp.dot(tri, hl_ref[p, 0], preferred_element_type=F32)
        total_new = cs_new[bk:bk + 1]
        w_ref[p, 1] = jnp.exp(z_ref[p, bk:, :] - cs_new[:bk]).astype(BF16)
        w_ref[p, 0] = jnp.exp(z_ref[p, :bk, :] - cs_old[:bk] - total_new).astype(BF16)
        carry_ref[p] = total_new + cs_old[bk:bk + 1]
    for p in range(n_pairs):
        acc_ref[p] = (jnp.dot(vt_ref[p, first], w_ref[p, 0], preferred_element_type=F32)
                      + jnp.dot(vt_ref[p, first + 1], w_ref[p, 1], preferred_element_type=F32))

    def least_carry():
        lows = [jnp.min(carry_ref[p]) for p in range(n_pairs)]
        return functools.reduce(jnp.minimum, lows)

    def cond(state):
        j, go = state
        return jnp.logical_and(j >= 0, go)

    def body(state):
        j, _ = state
        off = pl.multiple_of(j * bk, bk)
        for p in range(n_pairs):
            kj = k_ref[0, pl.ds(off, bk), pair_cols(p)]
            zt = jnp.dot(kj, qt_ref[p], preferred_element_type=F32)
            hi, lo = split_bf16(softplus(zt))
            cs = jnp.dot(tri, jnp.concatenate([hi, lo], axis=0), preferred_element_type=F32)
            wt = jnp.exp(zt - cs[:bk] - carry_ref[p]).astype(BF16)
            acc_ref[p] += jnp.dot(vt_ref[p, j], wt, preferred_element_type=F32)
            carry_ref[p] += cs[bk:bk + 1]
        return j - 1, least_carry() < SB_SKIP_LOG_MASS

    lax.while_loop(cond, body, (first - 1, least_carry() < SB_SKIP_LOG_MASS))

    feat = lax.broadcasted_iota(jnp.int32, (LANES, bq), 0)
    for p in range(n_pairs):
        res = acc_ref[p]
        ot = jnp.where(feat < HEAD_DIM, res[:, :bq], res[:, bq:])
        o_ref[0, :, pair_cols(p)] = ot.T.astype(o_ref.dtype)


def _suffix_sum_matrix(bk):
    s = lax.broadcasted_iota(jnp.int32, (bk + ONES_ROWS, 2 * bk), 0)
    j = lax.broadcasted_iota(jnp.int32, (bk + ONES_ROWS, 2 * bk), 1) % bk
    return jnp.logical_or(s >= bk, j >= s).astype(BF16)


def _sb_attention(qkv):
    b, s, _ = qkv.shape
    bq = SB_BLOCK
    pairs = SB_WIDTH // LANES
    tri = _suffix_sum_matrix(bq)
    return pl.pallas_call(
        _sb_kernel,
        grid=(b, s // bq),
        in_specs=[
            pl.BlockSpec((1, bq, SB_WIDTH), lambda bi, i: (bi, i, 0)),
            pl.BlockSpec((1, s, SB_WIDTH), lambda bi, i: (bi, 0, 1), pipeline_mode=pl.Buffered(1)),
            pl.BlockSpec((1, s, SB_WIDTH), lambda bi, i: (bi, 0, 2), pipeline_mode=pl.Buffered(1)),
            _whole(tri.shape),
        ],
        out_specs=pl.BlockSpec((1, bq, SB_WIDTH), lambda bi, i: (bi, i, 0)),
        out_shape=jax.ShapeDtypeStruct((b, s, SB_WIDTH), BF16),
        scratch_shapes=[
            pltpu.VMEM((pairs, s // bq, LANES, bq), BF16),
            pltpu.VMEM((pairs, LANES, 2 * bq), BF16),
            pltpu.VMEM((pairs, 2 * bq, 2 * bq), F32),
            pltpu.VMEM((pairs, 2, 2 * bq, 2 * bq), BF16),
            pltpu.VMEM((pairs, 2, bq, 2 * bq), BF16),
            pltpu.VMEM((pairs, 1, 2 * bq), F32),
            pltpu.VMEM((pairs, LANES, 2 * bq), F32),
        ],
        compiler_params=pltpu.CompilerParams(
            dimension_semantics=("arbitrary", "arbitrary"), vmem_limit_bytes=VMEM_LIMIT),
        name="stickbreak_attn",
    )(qkv, qkv, qkv, tri)


def _diff_kernel(lam_ref, g_ref, q_ref, k_ref, v_ref, o_ref, qt_ref, vt_ref, s_ref, p_ref, a_ref,
                 m_ref, acc_ref, *, lambda_init):
    bq = DIFF_BLOCK
    bk = bq
    n = q_ref.shape[1] // bq
    dv = DIFF_V_DIM
    lane = lax.broadcasted_iota(jnp.int32, (1, LANES), 1)
    lo_half = lane < HEAD_DIM

    def prepare(t, c):
        rows = pl.ds(pl.multiple_of(t * bq, bq), bq)
        q = q_ref[0, rows, :].astype(F32)
        qt_ref[t, 0] = jnp.where(lo_half, q, 0.0).T.astype(BF16)
        qt_ref[t, 1] = jnp.where(lo_half, 0.0, q).T.astype(BF16)
        vt_ref[t, :dv, :] = v_ref[0, rows, :].astype(F32).T.astype(BF16)
        vt_ref[t, dv:, :] = jnp.ones((ONES_ROWS, bk), BF16)
        m_ref[t] = jnp.full(m_ref.shape[1:], MASK_VALUE, F32)
        acc_ref[t] = jnp.zeros(acc_ref.shape[1:], F32)
        return c

    lax.fori_loop(0, n, prepare, 0, unroll=TILE_LOOP_UNROLL)

    def scores(i, j, slot):
        kj = k_ref[0, pl.ds(pl.multiple_of(j * bk, bk), bk), :]
        for mp in range(2):
            s_ref[slot, mp] = jnp.dot(kj, qt_ref[i, mp], preferred_element_type=F32)

    def softmax(i, slot, diagonal):
        for mp in range(2):
            st = s_ref[slot, mp]
            if diagonal:
                key = lax.broadcasted_iota(jnp.int32, (bk, bq), 0)
                qry = lax.broadcasted_iota(jnp.int32, (bk, bq), 1)
                st = jnp.where(key <= qry, st, MASK_VALUE)
            m_prev = m_ref[i, mp]
            m_new = jnp.maximum(m_prev, jnp.max(st, axis=0, keepdims=True))
            a_ref[slot, mp] = jnp.exp2(m_prev - m_new)
            p_ref[slot, mp] = jnp.exp2(st - m_new).astype(BF16)
            m_ref[i, mp] = m_new

    def accumulate(i, j, slot):
        for mp in range(2):
            pv = jnp.dot(vt_ref[j], p_ref[slot, mp], preferred_element_type=F32)
            acc_ref[i, mp] = a_ref[slot, mp] * acc_ref[i, mp] + pv

    def run(first, advance, n_steps, diagonal):
        scores(first[0], first[1], 0)
        p_ref[1] = jnp.zeros(p_ref.shape[1:], BF16)
        a_ref[1] = jnp.ones(a_ref.shape[1:], F32)

        def one(cur, prev, parity):
            nxt = advance(*cur)
            nxt = (jnp.minimum(nxt[0], n - 1), jnp.minimum(nxt[1], n - 1))
            scores(nxt[0], nxt[1], 1 - parity)
            accumulate(prev[0], prev[1], 1 - parity)
            softmax(cur[0], parity, diagonal)
            return nxt, cur

        def body(_, carry):
            cur, prev = carry
            for u in range(DIFF_UNROLL):
                cur, prev = one(cur, prev, u % 2)
            return cur, prev

        cur, prev = lax.fori_loop(0, n_steps // DIFF_UNROLL, body, (first, first))
        for u in range(n_steps % DIFF_UNROLL):
            cur, prev = one(cur, prev, u % 2)
        accumulate(prev[0], prev[1], (n_steps - 1) % 2)

    def next_below_diagonal(i, j):
        wrap = j + 1 >= i
        return jnp.where(wrap, i + 1, i), jnp.where(wrap, 0, j + 1)

    if n > 1:
        run((jnp.int32(1), jnp.int32(0)), next_below_diagonal, n * (n - 1) // 2, False)
    run((jnp.int32(0), jnp.int32(0)), lambda i, j: (i + 1, j + 1), n, True)

    lam_vecs = lam_ref[...]
    lam = (jnp.exp(jnp.sum(lam_vecs[0:1] * lam_vecs[1:2], axis=-1, keepdims=True))
           - jnp.exp(jnp.sum(lam_vecs[2:3] * lam_vecs[3:4], axis=-1, keepdims=True))
           + lambda_init)

    def finish(t, c):
        ot = (acc_ref[t, 0, :dv] / acc_ref[t, 0, dv:dv + 1]
              - lam * (acc_ref[t, 1, :dv] / acc_ref[t, 1, dv:dv + 1]))
        o = _rmsnorm_rows(ot.T, g_ref[...]) * (1.0 - lambda_init)
        o_ref[0, pl.ds(pl.multiple_of(t * bq, bq), bq), :] = o.astype(o_ref.dtype)
        return c

    lax.fori_loop(0, n, finish, 0, unroll=TILE_LOOP_UNROLL)


def _diff_attention(proj, lam_vecs, subln_g, lambda_init):
    b, s, _ = proj.shape
    bq = DIFF_BLOCK
    heads = DIFF_HEADS
    n = s // bq
    rows = DIFF_V_DIM + ONES_ROWS
    return pl.pallas_call(
        functools.partial(_diff_kernel, lambda_init=lambda_init),
        grid=(b, heads),
        in_specs=[
            _whole(lam_vecs.shape),
            _whole((1, DIFF_V_DIM)),
            pl.BlockSpec((1, s, LANES), lambda bi, h: (bi, 0, h)),
            pl.BlockSpec((1, s, LANES), lambda bi, h: (bi, 0, heads + h)),
            pl.BlockSpec((1, s, LANES), lambda bi, h: (bi, 0, 2 * heads + h)),
        ],
        out_specs=pl.BlockSpec((1, s, LANES), lambda bi, h: (bi, 0, h)),
        out_shape=jax.ShapeDtypeStruct((b, s, heads * DIFF_V_DIM), BF16),
        scratch_shapes=[
            pltpu.VMEM((n, 2, LANES, bq), BF16),
            pltpu.VMEM((n, rows, bq), BF16),
            pltpu.VMEM((2, 2, bq, bq), F32),
            pltpu.VMEM((2, 2, bq, bq), BF16),
            pltpu.VMEM((2, 2, 1, bq), F32),
            pltpu.VMEM((n, 2, 1, bq), F32),
            pltpu.VMEM((n, 2, rows, bq), F32),
        ],
        compiler_params=pltpu.CompilerParams(
            dimension_semantics=("arbitrary", "arbitrary"), vmem_limit_bytes=VMEM_LIMIT),
        name="diff_attn",
    )(lam_vecs, subln_g.reshape(1, DIFF_V_DIM), proj, proj, proj)


def _post_kernel(*refs, n_mix, final_norm):
    x_ref = refs[0]
    mix_refs = refs[1:1 + n_mix]
    wo_ref, g_ref, wu_ref, wd_ref = refs[1 + n_mix:5 + n_mix]
    gf_ref = refs[5 + n_mix] if final_norm else None
    o_ref, h_ref = refs[-2:]

    o_ref[...] = x_ref[...]
    off = 0
    for mref in mix_refs:
        width = mref.shape[1]
        o_ref[...] += jnp.dot(mref[...], wo_ref[off:off + width, :], preferred_element_type=F32)
        off += width
    h_ref[...] = _rmsnorm_rows(o_ref[...], g_ref[...]).astype(BF16)
    d_ff = wu_ref.shape[1]
    for c in range(0, d_ff, FF_CHUNK):
        up = jnp.dot(h_ref[...], wu_ref[:, c:c + FF_CHUNK], preferred_element_type=F32)
        act = jnp.square(jnp.maximum(up, 0.0)).astype(BF16)
        o_ref[...] += jnp.dot(act, wd_ref[c:c + FF_CHUNK, :], preferred_element_type=F32)
    if final_norm:
        o_ref[...] = _rmsnorm_rows(o_ref[...], gf_ref[...])


def _post(x, mixes, w_out, g_mlp, w_up, w_down, g_final):
    m, d = x.shape
    d_ff = w_up.shape[1]
    tm = ROW_TILE
    final_norm = g_final is not None
    in_specs = [pl.BlockSpec((tm, d), lambda i: (i, 0))]
    in_specs += [pl.BlockSpec((tm, mx.shape[1]), lambda i: (i, 0)) for mx in mixes]
    in_specs += [_whole(w_out.shape), _whole((1, d)), _whole((d, d_ff)), _whole((d_ff, d))]
    args = [x, *mixes, w_out.astype(BF16), g_mlp.reshape(1, d), w_up.astype(BF16),
            w_down.astype(BF16)]
    if final_norm:
        in_specs.append(_whole((1, d)))
        args.append(g_final.reshape(1, d))
    return pl.pallas_call(
        functools.partial(_post_kernel, n_mix=len(mixes), final_norm=final_norm),
        grid=(m // tm,),
        in_specs=in_specs,
        out_specs=pl.BlockSpec((tm, d), lambda i: (i, 0)),
        out_shape=jax.ShapeDtypeStruct((m, d), F32),
        scratch_shapes=[pltpu.VMEM((tm, d), BF16)],
        compiler_params=pltpu.CompilerParams(
            dimension_semantics=("arbitrary",), vmem_limit_bytes=VMEM_LIMIT),
        name="out_proj_mlp",
    )(*args)


def kernel(x, norm_mix, norm_mlp, norm_final, w_in_even, conv_w, w_out_even, w_in_odd,
           lam_q1, lam_k1, lam_q2, lam_k2, subln_g, w_out_odd, w_up, w_down):
    b, s, d = x.shape
    m = b * s
    depth = norm_mix.shape[0]
    xf = x.reshape(m, d)
    for layer in range(depth):
        i = layer // 2
        g_final = norm_final if layer == depth - 1 else None
        if layer % 2 == 0:
            qkv, c_out = _in_proj_even(xf, norm_mix[layer], w_in_even[i], conv_w[i], s)
            a_out = _sb_attention(qkv.reshape(b, s, 3 * SB_WIDTH))
            mixes = [a_out.reshape(m, SB_WIDTH), c_out]
            w_out = w_out_even[i]
        else:
            proj = _in_proj_odd(xf, norm_mix[layer], w_in_odd[i])
            lambda_init = 0.8 - 0.6 * math.exp(-0.3 * layer)
            lam_vecs = jnp.stack([lam_q1[i], lam_k1[i], lam_q2[i], lam_k2[i]])
            o = _diff_attention(proj.reshape(b, s, proj.shape[1]), lam_vecs, subln_g[i],
                                lambda_init)
            mixes = [o.reshape(m, DIFF_HEADS * DIFF_V_DIM)]
            w_out = w_out_odd[i]
        xf = _post(xf, mixes, w_out, norm_mlp[layer], w_up[layer], w_down[layer], g_final)
    return xf.reshape(b, s, d)
```

```python
import functools
import math

import jax
import jax.numpy as jnp
from jax import lax
from jax.experimental import pallas as pl
from jax.experimental.pallas import tpu as pltpu

F32 = jnp.float32
BF16 = jnp.bfloat16

NORM_EPS = 1e-6
HEAD_DIM = 64
SB_HEADS = 8
SB_WIDTH = SB_HEADS * HEAD_DIM
CONV_K = 3
DIFF_HEADS = 8
DIFF_V_DIM = 2 * HEAD_DIM
QK_SCALE = HEAD_DIM ** -0.5
LOG2E = math.log2(math.e)
ONES_ROWS = 16

LANES = 128
SUBLANES = 8
VMEM_LIMIT = 48 * 1024 * 1024

ROW_TILE = 512
FF_CHUNK = 1024
SB_BLOCK = 128
SB_TILES_PER_STEP = 2
DIFF_BLOCK = 256
DIFF_UNROLL = 32
TILE_LOOP_UNROLL = 8

SB_SKIP_LOG_MASS = 50.0

MASK_VALUE = -1e30


def _rmsnorm_rows(x, g):
    ms = jnp.mean(x * x, axis=-1, keepdims=True)
    return x * lax.rsqrt(ms + NORM_EPS) * g


def _whole(shape):
    return pl.BlockSpec(shape, lambda *_: (0,) * len(shape), pipeline_mode=pl.Buffered(1))


def _in_proj_even_kernel(x_ref, g_ref, w_ref, cw_ref, qkv_ref, c_ref, cu_ref, *, tiles_per_seq):
    tm = x_ref.shape[0]
    cw = c_ref.shape[1]

    @pl.when(pl.program_id(0) % tiles_per_seq == 0)
    def _():
        cu_ref[:SUBLANES, :] = jnp.zeros((SUBLANES, cw), F32)

    h = _rmsnorm_rows(x_ref[...], g_ref[...]).astype(BF16)
    base = 3 * SB_WIDTH
    c_gate = jnp.dot(h, w_ref[:, base + cw:base + 2 * cw], preferred_element_type=F32)
    u = jnp.dot(h, w_ref[:, base + 2 * cw:], preferred_element_type=F32)
    cu_ref[SUBLANES:, :] = c_gate * u
    y = cw_ref[0:1, :] * cu_ref[SUBLANES - 2:SUBLANES - 2 + tm, :]
    y = y + cw_ref[1:2, :] * cu_ref[SUBLANES - 1:SUBLANES - 1 + tm, :]
    y = y + cw_ref[2:3, :] * cu_ref[SUBLANES:, :]
    b_gate = jnp.dot(h, w_ref[:, base:base + cw], preferred_element_type=F32)
    c_ref[...] = (b_gate * y).astype(BF16)
    cu_ref[:SUBLANES, :] = cu_ref[tm:, :]
    q = jnp.dot(h, w_ref[:, :SB_WIDTH], preferred_element_type=F32) * QK_SCALE
    qkv_ref[:, :SB_WIDTH] = q.astype(BF16)
    kv = jnp.dot(h, w_ref[:, SB_WIDTH:3 * SB_WIDTH], preferred_element_type=F32)
    qkv_ref[:, SB_WIDTH:] = kv.astype(BF16)


def _in_proj_even(x, g, w, conv_w, seq):
    m, d = x.shape
    n = w.shape[1]
    cw = conv_w.shape[1]
    tm = ROW_TILE
    return pl.pallas_call(
        functools.partial(_in_proj_even_kernel, tiles_per_seq=seq // tm),
        grid=(m // tm,),
        in_specs=[
            pl.BlockSpec((tm, d), lambda i: (i, 0)),
            _whole((1, d)),
            _whole((d, n)),
            _whole((CONV_K, cw)),
        ],
        out_specs=[
            pl.BlockSpec((tm, 3 * SB_WIDTH), lambda i: (i, 0)),
            pl.BlockSpec((tm, cw), lambda i: (i, 0)),
        ],
        out_shape=[
            jax.ShapeDtypeStruct((m, 3 * SB_WIDTH), BF16),
            jax.ShapeDtypeStruct((m, cw), BF16),
        ],
        scratch_shapes=[pltpu.VMEM((tm + SUBLANES, cw), F32)],
        compiler_params=pltpu.CompilerParams(
            dimension_semantics=("arbitrary",), vmem_limit_bytes=VMEM_LIMIT),
        name="in_proj_even",
    )(x, g.reshape(1, d), w.astype(BF16), conv_w)


def _in_proj_odd_kernel(x_ref, g_ref, w_ref, o_ref, *, q_width):
    h = _rmsnorm_rows(x_ref[...], g_ref[...]).astype(BF16)
    q = jnp.dot(h, w_ref[:, :q_width], preferred_element_type=F32) * (QK_SCALE * LOG2E)
    o_ref[:, :q_width] = q.astype(BF16)
    kv = jnp.dot(h, w_ref[:, q_width:], preferred_element_type=F32)
    o_ref[:, q_width:] = kv.astype(BF16)


def _in_proj_odd(x, g, w):
    m, d = x.shape
    n = w.shape[1]
    tm = ROW_TILE
    return pl.pallas_call(
        functools.partial(_in_proj_odd_kernel, q_width=DIFF_HEADS * 2 * HEAD_DIM),
        grid=(m // tm,),
        in_specs=[
            pl.BlockSpec((tm, d), lambda i: (i, 0)),
            _whole((1, d)),
            _whole((d, n)),
        ],
        out_specs=pl.BlockSpec((tm, n), lambda i: (i, 0)),
        out_shape=jax.ShapeDtypeStruct((m, n), BF16),
        compiler_params=pltpu.CompilerParams(
            dimension_semantics=("arbitrary",), vmem_limit_bytes=VMEM_LIMIT),
        name="in_proj_odd",
    )(x, g.reshape(1, d), w.astype(BF16))


def _sb_kernel(q_ref, k_ref, v_ref, tri_ref, o_ref, vt_ref, qt_ref, z_ref, hl_ref, w_ref,
               carry_ref, acc_ref):
    bq = SB_BLOCK
    bk = bq
    n_tiles = q_ref.shape[1] // bq
    n_pairs, n_blocks = vt_ref.shape[:2]
    step = pl.program_id(1)
    chains = [(u, p) for u in range(n_tiles) for p in range(n_pairs)]

    def pair_cols(p):
        return slice(p * LANES, (p + 1) * LANES)

    @pl.when(step == 0)
    def _():
        def tr(j, c):
            off = pl.multiple_of(j * bk, bk)
            for p in range(n_pairs):
                vj = v_ref[0, pl.ds(off, bk), pair_cols(p)]
                vt_ref[p, j] = vj.astype(F32).T.astype(BF16)
            return c
        lax.fori_loop(0, n_blocks, tr, 0)

    lane = lax.broadcasted_iota(jnp.int32, (1, LANES), 1)
    lo_half = lane < HEAD_DIM
    for u, p in chains:
        q = q_ref[0, u * bq:(u + 1) * bq, pair_cols(p)].astype(F32)
        qt_ref[u, p, :, :bq] = jnp.where(lo_half, q, 0.0).T.astype(BF16)
        qt_ref[u, p, :, bq:] = jnp.where(lo_half, 0.0, q).T.astype(BF16)

    tri = tri_ref[...]

    def softplus(zt):
        return jnp.maximum(zt, 0.0) + jnp.log(1.0 + jnp.exp(-jnp.abs(zt)))

    def split_bf16(sp):
        hi = sp.astype(BF16)
        return hi, (sp - hi.astype(F32)).astype(BF16)

    tile = [step * n_tiles + u for u in range(n_tiles)]
    first = [jnp.maximum(i - 1, 0) for i in tile]
    key = lax.broadcasted_iota(jnp.int32, (2 * bk, 2 * bq), 0)
    qry = lax.broadcasted_iota(jnp.int32, (2 * bk, 2 * bq), 1) % bq
    for u, p in chains:
        keep = key - qry < (tile[u] - first[u]) * bq
        k2 = k_ref[0, pl.ds(pl.multiple_of(first[u] * bk, bk), 2 * bk), pair_cols(p)]
        zt = jnp.dot(k2, qt_ref[u, p], preferred_element_type=F32)
        zt = jnp.where(keep, zt, MASK_VALUE)
        z_ref[u, p] = zt
        hi, lo = split_bf16(softplus(zt))
        for half in range(2):
            rows = slice(half * bk, (half + 1) * bk)
            hl_ref[u, p, half] = jnp.concatenate([hi[rows], lo[rows]], axis=0)
    for u, p in chains:
        cs_new = jnp.dot(tri, hl_ref[u, p, 1], preferred_element_type=F32)
        cs_old = jnp.dot(tri, hl_ref[u, p, 0], preferred_element_type=F32)
        total_new = cs_new[bk:bk + 1]
        w_ref[u, p, 1] = jnp.exp(z_ref[u, p, bk:, :] - cs_new[:bk]).astype(BF16)
        w_ref[u, p, 0] = jnp.exp(z_ref[u, p, :bk, :] - cs_old[:bk] - total_new).astype(BF16)
        carry_ref[u, p] = total_new + cs_old[bk:bk + 1]
    for u, p in chains:
        acc_ref[u, p] = (
            jnp.dot(vt_ref[p, first[u]], w_ref[u, p, 0], preferred_element_type=F32)
            + jnp.dot(vt_ref[p, first[u] + 1], w_ref[u, p, 1], preferred_element_type=F32))

    for u in range(n_tiles):
        def least_carry(u=u):
            lows = [jnp.min(carry_ref[u, p]) for p in range(n_pairs)]
            return functools.reduce(jnp.minimum, lows)

        def cond(state):
            j, go = state
            return jnp.logical_and(j >= 0, go)

        def body(state, u=u, least_carry=least_carry):
            j, _ = state
            off = pl.multiple_of(j * bk, bk)
            for p in range(n_pairs):
                kj = k_ref[0, pl.ds(off, bk), pair_cols(p)]
                zt = jnp.dot(kj, qt_ref[u, p], preferred_element_type=F32)
                hi, lo = split_bf16(softplus(zt))
                cs = jnp.dot(tri, jnp.concatenate([hi, lo], axis=0),
                             preferred_element_type=F32)
                wt = jnp.exp(zt - cs[:bk] - carry_ref[u, p]).astype(BF16)
                acc_ref[u, p] += jnp.dot(vt_ref[p, j], wt, preferred_element_type=F32)
                carry_ref[u, p] += cs[bk:bk + 1]
            return j - 1, least_carry() < SB_SKIP_LOG_MASS

        lax.while_loop(cond, body, (first[u] - 1, least_carry() < SB_SKIP_LOG_MASS))

    feat = lax.broadcasted_iota(jnp.int32, (LANES, bq), 0)
    for u, p in chains:
        res = acc_ref[u, p]
        ot = jnp.where(feat < HEAD_DIM, res[:, :bq], res[:, bq:])
        o_ref[0, u * bq:(u + 1) * bq, pair_cols(p)] = ot.T.astype(o_ref.dtype)


def _suffix_sum_matrix(bk):
    s = lax.broadcasted_iota(jnp.int32, (bk + ONES_ROWS, 2 * bk), 0)
    j = lax.broadcasted_iota(jnp.int32, (bk + ONES_ROWS, 2 * bk), 1) % bk
    return jnp.logical_or(s >= bk, j >= s).astype(BF16)


def _sb_attention(qkv):
    b, s, _ = qkv.shape
    bq = SB_BLOCK
    tiles = SB_TILES_PER_STEP
    pairs = SB_WIDTH // LANES
    tri = _suffix_sum_matrix(bq)
    return pl.pallas_call(
        _sb_kernel,
        grid=(b, s // (tiles * bq)),
        in_specs=[
            pl.BlockSpec((1, tiles * bq, SB_WIDTH), lambda bi, i: (bi, i, 0)),
            pl.BlockSpec((1, s, SB_WIDTH), lambda bi, i: (bi, 0, 1), pipeline_mode=pl.Buffered(1)),
            pl.BlockSpec((1, s, SB_WIDTH), lambda bi, i: (bi, 0, 2), pipeline_mode=pl.Buffered(1)),
            _whole(tri.shape),
        ],
        out_specs=pl.BlockSpec((1, tiles * bq, SB_WIDTH), lambda bi, i: (bi, i, 0)),
        out_shape=jax.ShapeDtypeStruct((b, s, SB_WIDTH), BF16),
        scratch_shapes=[
            pltpu.VMEM((pairs, s // bq, LANES, bq), BF16),
            pltpu.VMEM((tiles, pairs, LANES, 2 * bq), BF16),
            pltpu.VMEM((tiles, pairs, 2 * bq, 2 * bq), F32),
            pltpu.VMEM((tiles, pairs, 2, 2 * bq, 2 * bq), BF16),
            pltpu.VMEM((tiles, pairs, 2, bq, 2 * bq), BF16),
            pltpu.VMEM((tiles, pairs, 1, 2 * bq), F32),
            pltpu.VMEM((tiles, pairs, LANES, 2 * bq), F32),
        ],
        compiler_params=pltpu.CompilerParams(
            dimension_semantics=("arbitrary", "arbitrary"), vmem_limit_bytes=VMEM_LIMIT),
        name="stickbreak_attn",
    )(qkv, qkv, qkv, tri)


def _diff_kernel(lam_ref, g_ref, q_ref, k_ref, v_ref, o_ref, qt_ref, vt_ref, s_ref, p_ref, a_ref,
                 m_ref, acc_ref, *, lambda_init):
    bq = DIFF_BLOCK
    bk = bq
    n = q_ref.shape[1] // bq
    dv = DIFF_V_DIM
    lane = lax.broadcasted_iota(jnp.int32, (1, LANES), 1)
    lo_half = lane < HEAD_DIM

    def prepare(t, c):
        rows = pl.ds(pl.multiple_of(t * bq, bq), bq)
        q = q_ref[0, rows, :].astype(F32)
        qt_ref[t, 0] = jnp.where(lo_half, q, 0.0).T.astype(BF16)
        qt_ref[t, 1] = jnp.where(lo_half, 0.0, q).T.astype(BF16)
        vt_ref[t, :dv, :] = v_ref[0, rows, :].astype(F32).T.astype(BF16)
        vt_ref[t, dv:, :] = jnp.ones((ONES_ROWS, bk), BF16)
        m_ref[t] = jnp.full(m_ref.shape[1:], MASK_VALUE, F32)
        acc_ref[t] = jnp.zeros(acc_ref.shape[1:], F32)
        return c

    lax.fori_loop(0, n, prepare, 0, unroll=TILE_LOOP_UNROLL)

    def scores(i, j, slot):
        kj = k_ref[0, pl.ds(pl.multiple_of(j * bk, bk), bk), :]
        for mp in range(2):
            s_ref[slot, mp] = jnp.dot(kj, qt_ref[i, mp], preferred_element_type=F32)

    def softmax(i, slot, diagonal):
        for mp in range(2):
            st = s_ref[slot, mp]
            if diagonal:
                key = lax.broadcasted_iota(jnp.int32, (bk, bq), 0)
                qry = lax.broadcasted_iota(jnp.int32, (bk, bq), 1)
                st = jnp.where(key <= qry, st, MASK_VALUE)
            m_prev = m_ref[i, mp]
            m_new = jnp.maximum(m_prev, jnp.max(st, axis=0, keepdims=True))
            a_ref[slot, mp] = jnp.exp2(m_prev - m_new)
            p_ref[slot, mp] = jnp.exp2(st - m_new).astype(BF16)
            m_ref[i, mp] = m_new

    def accumulate(i, j, slot):
        for mp in range(2):
            pv = jnp.dot(vt_ref[j], p_ref[slot, mp], preferred_element_type=F32)
            acc_ref[i, mp] = a_ref[slot, mp] * acc_ref[i, mp] + pv

    def run(first, advance, n_steps, diagonal):
        scores(first[0], first[1], 0)
        p_ref[1] = jnp.zeros(p_ref.shape[1:], BF16)
        a_ref[1] = jnp.ones(a_ref.shape[1:], F32)

        def one(cur, prev, parity):
            nxt = advance(*cur)
            nxt = (jnp.minimum(nxt[0], n - 1), jnp.minimum(nxt[1], n - 1))
            scores(nxt[0], nxt[1], 1 - parity)
            accumulate(prev[0], prev[1], 1 - parity)
            softmax(cur[0], parity, diagonal)
            return nxt, cur

        def body(_, carry):
            cur, prev = carry
            for u in range(DIFF_UNROLL):
                cur, prev = one(cur, prev, u % 2)
            return cur, prev

        cur, prev = lax.fori_loop(0, n_steps // DIFF_UNROLL, body, (first, first))
        for u in range(n_steps % DIFF_UNROLL):
            cur, prev = one(cur, prev, u % 2)
        accumulate(prev[0], prev[1], (n_steps - 1) % 2)

    def next_below_diagonal(i, j):
        wrap = j + 1 >= i
        return jnp.where(wrap, i + 1, i), jnp.where(wrap, 0, j + 1)

    if n > 1:
        run((jnp.int32(1), jnp.int32(0)), next_below_diagonal, n * (n - 1) // 2, False)
    run((jnp.int32(0), jnp.int32(0)), lambda i, j: (i + 1, j + 1), n, True)

    lam_vecs = lam_ref[...]
    lam = (jnp.exp(jnp.sum(lam_vecs[0:1] * lam_vecs[1:2], axis=-1, keepdims=True))
           - jnp.exp(jnp.sum(lam_vecs[2:3] * lam_vecs[3:4], axis=-1, keepdims=True))
           + lambda_init)

    def finish(t, c):
        ot = (acc_ref[t, 0, :dv] / acc_ref[t, 0, dv:dv + 1]
              - lam * (acc_ref[t, 1, :dv] / acc_ref[t, 1, dv:dv + 1]))
        o = _rmsnorm_rows(ot.T, g_ref[...]) * (1.0 - lambda_init)
        o_ref[0, pl.ds(pl.multiple_of(t * bq, bq), bq), :] = o.astype(o_ref.dtype)
        return c

    lax.fori_loop(0, n, finish, 0, unroll=TILE_LOOP_UNROLL)


def _diff_attention(proj, lam_vecs, subln_g, lambda_init):
    b, s, _ = proj.shape
    bq = DIFF_BLOCK
    heads = DIFF_HEADS
    n = s // bq
    rows = DIFF_V_DIM + ONES_ROWS
    return pl.pallas_call(
        functools.partial(_diff_kernel, lambda_init=lambda_init),
        grid=(b, heads),
        in_specs=[
            _whole(lam_vecs.shape),
            _whole((1, DIFF_V_DIM)),
            pl.BlockSpec((1, s, LANES), lambda bi, h: (bi, 0, h)),
            pl.BlockSpec((1, s, LANES), lambda bi, h: (bi, 0, heads + h)),
            pl.BlockSpec((1, s, LANES), lambda bi, h: (bi, 0, 2 * heads + h)),
        ],
        out_specs=pl.BlockSpec((1, s, LANES), lambda bi, h: (bi, 0, h)),
        out_shape=jax.ShapeDtypeStruct((b, s, heads * DIFF_V_DIM), BF16),
        scratch_shapes=[
            pltpu.VMEM((n, 2, LANES, bq), BF16),
            pltpu.VMEM((n, rows, bq), BF16),
            pltpu.VMEM((2, 2, bq, bq), F32),
            pltpu.VMEM((2, 2, bq, bq), BF16),
            pltpu.VMEM((2, 2, 1, bq), F32),
            pltpu.VMEM((n, 2, 1, bq), F32),
            pltpu.VMEM((n, 2, rows, bq), F32),
        ],
        compiler_params=pltpu.CompilerParams(
            dimension_semantics=("arbitrary", "arbitrary"), vmem_limit_bytes=VMEM_LIMIT),
        name="diff_attn",
    )(lam_vecs, subln_g.reshape(1, DIFF_V_DIM), proj, proj, proj)


def _post_kernel(*refs, n_mix, final_norm):
    x_ref = refs[0]
    mix_refs = refs[1:1 + n_mix]
    wo_ref, g_ref, wu_ref, wd_ref = refs[1 + n_mix:5 + n_mix]
    gf_ref = refs[5 + n_mix] if final_norm else None
    o_ref, h_ref = refs[-2:]

    o_ref[...] = x_ref[...]
    off = 0
    for mref in mix_refs:
        width = mref.shape[1]
        o_ref[...] += jnp.dot(mref[...], wo_ref[off:off + width, :], preferred_element_type=F32)
        off += width
    h_ref[...] = _rmsnorm_rows(o_ref[...], g_ref[...]).astype(BF16)
    d_ff = wu_ref.shape[1]
    for c in range(0, d_ff, FF_CHUNK):
        up = jnp.dot(h_ref[...], wu_ref[:, c:c + FF_CHUNK], preferred_element_type=F32)
        act = jnp.square(jnp.maximum(up, 0.0)).astype(BF16)
        o_ref[...] += jnp.dot(act, wd_ref[c:c + FF_CHUNK, :], preferred_element_type=F32)
    if final_norm:
        o_ref[...] = _rmsnorm_rows(o_ref[...], gf_ref[...])


def _post(x, mixes, w_out, g_mlp, w_up, w_down, g_final):
    m, d = x.shape
    d_ff = w_up.shape[1]
    tm = ROW_TILE
    final_norm = g_final is not None
    in_specs = [pl.BlockSpec((tm, d), lambda i: (i, 0))]
    in_specs += [pl.BlockSpec((tm, mx.shape[1]), lambda i: (i, 0)) for mx in mixes]
    in_specs += [_whole(w_out.shape), _whole((1, d)), _whole((d, d_ff)), _whole((d_ff, d))]
    args = [x, *mixes, w_out.astype(BF16), g_mlp.reshape(1, d), w_up.astype(BF16),
            w_down.astype(BF16)]
    if final_norm:
        in_specs.append(_whole((1, d)))
        args.append(g_final.reshape(1, d))
    return pl.pallas_call(
        functools.partial(_post_kernel, n_mix=len(mixes), final_norm=final_norm),
        grid=(m // tm,),
        in_specs=in_specs,
        out_specs=pl.BlockSpec((tm, d), lambda i: (i, 0)),
        out_shape=jax.ShapeDtypeStruct((m, d), F32),
        scratch_shapes=[pltpu.VMEM((tm, d), BF16)],
        compiler_params=pltpu.CompilerParams(
            dimension_semantics=("arbitrary",), vmem_limit_bytes=VMEM_LIMIT),
        name="out_proj_mlp",
    )(*args)


def kernel(x, norm_mix, norm_mlp, norm_final, w_in_even, conv_w, w_out_even, w_in_odd,
           lam_q1, lam_k1, lam_q2, lam_k2, subln_g, w_out_odd, w_up, w_down):
    b, s, d = x.shape
    m = b * s
    depth = norm_mix.shape[0]
    xf = x.reshape(m, d)
    for layer in range(depth):
        i = layer // 2
        g_final = norm_final if layer == depth - 1 else None
        if layer % 2 == 0:
            qkv, c_out = _in_proj_even(xf, norm_mix[layer], w_in_even[i], conv_w[i], s)
            a_out = _sb_attention(qkv.reshape(b, s, 3 * SB_WIDTH))
            mixes = [a_out.reshape(m, SB_WIDTH), c_out]
            w_out = w_out_even[i]
        else:
            proj = _in_proj_odd(xf, norm_mix[layer], w_in_odd[i])
            lambda_init = 0.8 - 0.6 * math.exp(-0.3 * layer)
            lam_vecs = jnp.stack([lam_q1[i], lam_k1[i], lam_q2[i], lam_k2[i]])
            o = _diff_attention(proj.reshape(b, s, proj.shape[1]), lam_vecs, subln_g[i],
                                lambda_init)
            mixes = [o.reshape(m, DIFF_HEADS * DIFF_V_DIM)]
            w_out = w_out_odd[i]
        xf = _post(xf, mixes, w_out, norm_mlp[layer], w_up[layer], w_down[layer], g_final)
    return xf.reshape(b, s, d)
```

```python
import functools
import math

import jax
import jax.numpy as jnp
from jax import lax
from jax.experimental import pallas as pl
from jax.experimental.pallas import tpu as pltpu

F32 = jnp.float32
BF16 = jnp.bfloat16

NORM_EPS = 1e-6
HEAD_DIM = 64
SB_HEADS = 8
SB_WIDTH = SB_HEADS * HEAD_DIM
CONV_K = 3
DIFF_HEADS = 8
DIFF_V_DIM = 2 * HEAD_DIM
QK_SCALE = HEAD_DIM ** -0.5
LOG2E = math.log2(math.e)
ONES_ROWS = 16

LANES = 128
SUBLANES = 8
VMEM_LIMIT = 48 * 1024 * 1024

ROW_TILE = 512
IN_PROJ_ROW_TILE = 1024
FF_CHUNK = 1024
SB_BLOCK = 128
SB_TILES_PER_STEP = 2
DIFF_BLOCK = 256
DIFF_UNROLL = 32
TILE_LOOP_UNROLL = 8

SB_SKIP_LOG_MASS = 50.0

MASK_VALUE = -1e30


def _rmsnorm_rows(x, g):
    ms = jnp.mean(x * x, axis=-1, keepdims=True)
    return x * lax.rsqrt(ms + NORM_EPS) * g


def _whole(shape):
    return pl.BlockSpec(shape, lambda *_: (0,) * len(shape), pipeline_mode=pl.Buffered(1))


def _in_proj_even_kernel(x_ref, g_ref, w_ref, cw_ref, qkv_ref, c_ref, cu_ref, *, tiles_per_seq):
    tm = x_ref.shape[0]
    cw = c_ref.shape[1]

    @pl.when(pl.program_id(0) % tiles_per_seq == 0)
    def _():
        cu_ref[:SUBLANES, :] = jnp.zeros((SUBLANES, cw), F32)

    h = _rmsnorm_rows(x_ref[...], g_ref[...]).astype(BF16)
    base = 3 * SB_WIDTH
    c_gate = jnp.dot(h, w_ref[:, base + cw:base + 2 * cw], preferred_element_type=F32)
    u = jnp.dot(h, w_ref[:, base + 2 * cw:], preferred_element_type=F32)
    cu_ref[SUBLANES:, :] = c_gate * u
    y = cw_ref[0:1, :] * cu_ref[SUBLANES - 2:SUBLANES - 2 + tm, :]
    y = y + cw_ref[1:2, :] * cu_ref[SUBLANES - 1:SUBLANES - 1 + tm, :]
    y = y + cw_ref[2:3, :] * cu_ref[SUBLANES:, :]
    b_gate = jnp.dot(h, w_ref[:, base:base + cw], preferred_element_type=F32)
    c_ref[...] = (b_gate * y).astype(BF16)
    cu_ref[:SUBLANES, :] = cu_ref[tm:, :]
    q = jnp.dot(h, w_ref[:, :SB_WIDTH], preferred_element_type=F32) * (QK_SCALE * LOG2E)
    qkv_ref[:, :SB_WIDTH] = q.astype(BF16)
    kv = jnp.dot(h, w_ref[:, SB_WIDTH:3 * SB_WIDTH], preferred_element_type=F32)
    qkv_ref[:, SB_WIDTH:] = kv.astype(BF16)


def _in_proj_even(x, g, w, conv_w, seq):
    m, d = x.shape
    n = w.shape[1]
    cw = conv_w.shape[1]
    tm = IN_PROJ_ROW_TILE
    return pl.pallas_call(
        functools.partial(_in_proj_even_kernel, tiles_per_seq=seq // tm),
        grid=(m // tm,),
        in_specs=[
            pl.BlockSpec((tm, d), lambda i: (i, 0)),
            _whole((1, d)),
            _whole((d, n)),
            _whole((CONV_K, cw)),
        ],
        out_specs=[
            pl.BlockSpec((tm, 3 * SB_WIDTH), lambda i: (i, 0)),
            pl.BlockSpec((tm, cw), lambda i: (i, 0)),
        ],
        out_shape=[
            jax.ShapeDtypeStruct((m, 3 * SB_WIDTH), BF16),
            jax.ShapeDtypeStruct((m, cw), BF16),
        ],
        scratch_shapes=[pltpu.VMEM((tm + SUBLANES, cw), F32)],
        compiler_params=pltpu.CompilerParams(
            dimension_semantics=("arbitrary",), vmem_limit_bytes=VMEM_LIMIT),
        name="in_proj_even",
    )(x, g.reshape(1, d), w.astype(BF16), conv_w)


def _in_proj_odd_kernel(x_ref, g_ref, w_ref, o_ref, *, q_width):
    h = _rmsnorm_rows(x_ref[...], g_ref[...]).astype(BF16)
    q = jnp.dot(h, w_ref[:, :q_width], preferred_element_type=F32) * (QK_SCALE * LOG2E)
    o_ref[:, :q_width] = q.astype(BF16)
    kv = jnp.dot(h, w_ref[:, q_width:], preferred_element_type=F32)
    o_ref[:, q_width:] = kv.astype(BF16)


def _in_proj_odd(x, g, w):
    m, d = x.shape
    n = w.shape[1]
    tm = IN_PROJ_ROW_TILE
    return pl.pallas_call(
        functools.partial(_in_proj_odd_kernel, q_width=DIFF_HEADS * 2 * HEAD_DIM),
        grid=(m // tm,),
        in_specs=[
            pl.BlockSpec((tm, d), lambda i: (i, 0)),
            _whole((1, d)),
            _whole((d, n)),
        ],
        out_specs=pl.BlockSpec((tm, n), lambda i: (i, 0)),
        out_shape=jax.ShapeDtypeStruct((m, n), BF16),
        compiler_params=pltpu.CompilerParams(
            dimension_semantics=("arbitrary",), vmem_limit_bytes=VMEM_LIMIT),
        name="in_proj_odd",
    )(x, g.reshape(1, d), w.astype(BF16))


def _sb_kernel(q_ref, k_ref, v_ref, tri_ref, o_ref, vt_ref, qt_ref, z_ref, hl_ref, w_ref,
               carry_ref, acc_ref):
    bq = SB_BLOCK
    bk = bq
    n_tiles = q_ref.shape[1] // bq
    n_pairs, n_blocks = vt_ref.shape[:2]
    step = pl.program_id(1)
    chains = [(u, p) for u in range(n_tiles) for p in range(n_pairs)]

    def pair_cols(p):
        return slice(p * LANES, (p + 1) * LANES)

    @pl.when(step == 0)
    def _():
        def tr(j, c):
            off = pl.multiple_of(j * bk, bk)
            for p in range(n_pairs):
                vj = v_ref[0, pl.ds(off, bk), pair_cols(p)]
                vt_ref[p, j] = vj.astype(F32).T.astype(BF16)
            return c
        lax.fori_loop(0, n_blocks, tr, 0)

    lane = lax.broadcasted_iota(jnp.int32, (1, LANES), 1)
    lo_half = lane < HEAD_DIM
    for u, p in chains:
        q = q_ref[0, u * bq:(u + 1) * bq, pair_cols(p)].astype(F32)
        qt_ref[u, p, :, :bq] = jnp.where(lo_half, q, 0.0).T.astype(BF16)
        qt_ref[u, p, :, bq:] = jnp.where(lo_half, 0.0, q).T.astype(BF16)

    tri = tri_ref[...]

    def softplus(zt):
        return jnp.maximum(zt, 0.0) + jnp.log(1.0 + jnp.exp2(-jnp.abs(zt))) * LOG2E

    def split_bf16(sp):
        hi = sp.astype(BF16)
        return hi, (sp - hi.astype(F32)).astype(BF16)

    tile = [step * n_tiles + u for u in range(n_tiles)]
    first = [jnp.maximum(i - 1, 0) for i in tile]
    key = lax.broadcasted_iota(jnp.int32, (2 * bk, 2 * bq), 0)
    qry = lax.broadcasted_iota(jnp.int32, (2 * bk, 2 * bq), 1) % bq
    bias = [jnp.where(key - qry < (tile[u] - first[u]) * bq, 0.0, MASK_VALUE)
            for u in range(n_tiles)]
    for u, p in chains:
        k2 = k_ref[0, pl.ds(pl.multiple_of(first[u] * bk, bk), 2 * bk), pair_cols(p)]
        zt = jnp.dot(k2, qt_ref[u, p], preferred_element_type=F32)
        zt = zt + bias[u]
        z_ref[u, p] = zt
        hi, lo = split_bf16(softplus(zt))
        for half in range(2):
            rows = slice(half * bk, (half + 1) * bk)
            hl_ref[u, p, half] = jnp.concatenate([hi[rows], lo[rows]], axis=0)
    for u, p in chains:
        cs_new = jnp.dot(tri, hl_ref[u, p, 1], preferred_element_type=F32)
        cs_old = jnp.dot(tri, hl_ref[u, p, 0], preferred_element_type=F32)
        total_new = cs_new[bk:bk + 1]
        w_ref[u, p, 1] = jnp.exp2(z_ref[u, p, bk:, :] - cs_new[:bk]).astype(BF16)
        w_ref[u, p, 0] = jnp.exp2(z_ref[u, p, :bk, :] - cs_old[:bk] - total_new).astype(BF16)
        carry_ref[u, p] = total_new + cs_old[bk:bk + 1]
    for u, p in chains:
        acc_ref[u, p] = (
            jnp.dot(vt_ref[p, first[u]], w_ref[u, p, 0], preferred_element_type=F32)
            + jnp.dot(vt_ref[p, first[u] + 1], w_ref[u, p, 1], preferred_element_type=F32))

    for u in range(n_tiles):
        def least_carry(u=u):
            lows = [jnp.min(carry_ref[u, p]) for p in range(n_pairs)]
            return functools.reduce(jnp.minimum, lows)

        def cond(state):
            j, go = state
            return jnp.logical_and(j >= 0, go)

        def body(state, u=u, least_carry=least_carry):
            j, _ = state
            off = pl.multiple_of(j * bk, bk)
            for p in range(n_pairs):
                kj = k_ref[0, pl.ds(off, bk), pair_cols(p)]
                zt = jnp.dot(kj, qt_ref[u, p], preferred_element_type=F32)
                hi, lo = split_bf16(softplus(zt))
                cs = jnp.dot(tri, jnp.concatenate([hi, lo], axis=0),
                             preferred_element_type=F32)
                wt = jnp.exp2(zt - cs[:bk] - carry_ref[u, p]).astype(BF16)
                acc_ref[u, p] += jnp.dot(vt_ref[p, j], wt, preferred_element_type=F32)
                carry_ref[u, p] += cs[bk:bk + 1]
            return j - 1, least_carry() < SB_SKIP_LOG_MASS * LOG2E

        lax.while_loop(cond, body, (first[u] - 1, least_carry() < SB_SKIP_LOG_MASS * LOG2E))

    feat = lax.broadcasted_iota(jnp.int32, (LANES, bq), 0)
    for u, p in chains:
        res = acc_ref[u, p]
        ot = jnp.where(feat < HEAD_DIM, res[:, :bq], res[:, bq:])
        o_ref[0, u * bq:(u + 1) * bq, pair_cols(p)] = ot.T.astype(o_ref.dtype)


def _suffix_sum_matrix(bk):
    s = lax.broadcasted_iota(jnp.int32, (bk + ONES_ROWS, 2 * bk), 0)
    j = lax.broadcasted_iota(jnp.int32, (bk + ONES_ROWS, 2 * bk), 1) % bk
    return jnp.logical_or(s >= bk, j >= s).astype(BF16)


def _sb_attention(qkv):
    b, s, _ = qkv.shape
    bq = SB_BLOCK
    tiles = SB_TILES_PER_STEP
    pairs = SB_WIDTH // LANES
    tri = _suffix_sum_matrix(bq)
    return pl.pallas_call(
        _sb_kernel,
        grid=(b, s // (tiles * bq)),
        in_specs=[
            pl.BlockSpec((1, tiles * bq, SB_WIDTH), lambda bi, i: (bi, i, 0)),
            pl.BlockSpec((1, s, SB_WIDTH), lambda bi, i: (bi, 0, 1), pipeline_mode=pl.Buffered(1)),
            pl.BlockSpec((1, s, SB_WIDTH), lambda bi, i: (bi, 0, 2), pipeline_mode=pl.Buffered(1)),
            _whole(tri.shape),
        ],
        out_specs=pl.BlockSpec((1, tiles * bq, SB_WIDTH), lambda bi, i: (bi, i, 0)),
        out_shape=jax.ShapeDtypeStruct((b, s, SB_WIDTH), BF16),
        scratch_shapes=[
            pltpu.VMEM((pairs, s // bq, LANES, bq), BF16),
            pltpu.VMEM((tiles, pairs, LANES, 2 * bq), BF16),
            pltpu.VMEM((tiles, pairs, 2 * bq, 2 * bq), F32),
            pltpu.VMEM((tiles, pairs, 2, 2 * bq, 2 * bq), BF16),
            pltpu.VMEM((tiles, pairs, 2, bq, 2 * bq), BF16),
            pltpu.VMEM((tiles, pairs, 1, 2 * bq), F32),
            pltpu.VMEM((tiles, pairs, LANES, 2 * bq), F32),
        ],
        compiler_params=pltpu.CompilerParams(
            dimension_semantics=("arbitrary", "arbitrary"), vmem_limit_bytes=VMEM_LIMIT),
        name="stickbreak_attn",
    )(qkv, qkv, qkv, tri)


def _diff_kernel(lam_ref, g_ref, q_ref, k_ref, v_ref, o_ref, qt_ref, vt_ref, s_ref, p_ref, a_ref,
                 m_ref, acc_ref, *, lambda_init):
    bq = DIFF_BLOCK
    bk = bq
    n = q_ref.shape[1] // bq
    dv = DIFF_V_DIM
    lane = lax.broadcasted_iota(jnp.int32, (1, LANES), 1)
    lo_half = lane < HEAD_DIM

    def prepare(t, c):
        rows = pl.ds(pl.multiple_of(t * bq, bq), bq)
        q = q_ref[0, rows, :].astype(F32)
        qt_ref[t, 0] = jnp.where(lo_half, q, 0.0).T.astype(BF16)
        qt_ref[t, 1] = jnp.where(lo_half, 0.0, q).T.astype(BF16)
        vt_ref[t, :dv, :] = v_ref[0, rows, :].astype(F32).T.astype(BF16)
        vt_ref[t, dv:, :] = jnp.ones((ONES_ROWS, bk), BF16)
        m_ref[t] = jnp.full(m_ref.shape[1:], MASK_VALUE, F32)
        acc_ref[t] = jnp.zeros(acc_ref.shape[1:], F32)
        return c

    lax.fori_loop(0, n, prepare, 0, unroll=TILE_LOOP_UNROLL)

    def scores(i, j, slot):
        kj = k_ref[0, pl.ds(pl.multiple_of(j * bk, bk), bk), :]
        for mp in range(2):
            s_ref[slot, mp] = jnp.dot(kj, qt_ref[i, mp], preferred_element_type=F32)

    def softmax(i, slot, diagonal):
        for mp in range(2):
            st = s_ref[slot, mp]
            if diagonal:
                key = lax.broadcasted_iota(jnp.int32, (bk, bq), 0)
                qry = lax.broadcasted_iota(jnp.int32, (bk, bq), 1)
                st = jnp.where(key <= qry, st, MASK_VALUE)
            m_prev = m_ref[i, mp]
            m_new = jnp.maximum(m_prev, jnp.max(st, axis=0, keepdims=True))
            a_ref[slot, mp] = jnp.exp2(m_prev - m_new)
            p_ref[slot, mp] = jnp.exp2(st - m_new).astype(BF16)
            m_ref[i, mp] = m_new

    def accumulate(i, j, slot):
        for mp in range(2):
            pv = jnp.dot(vt_ref[j], p_ref[slot, mp], preferred_element_type=F32)
            acc_ref[i, mp] = a_ref[slot, mp] * acc_ref[i, mp] + pv

    def run(first, advance, n_steps, diagonal):
        scores(first[0], first[1], 0)
        p_ref[1] = jnp.zeros(p_ref.shape[1:], BF16)
        a_ref[1] = jnp.ones(a_ref.shape[1:], F32)

        def one(cur, prev, parity):
            nxt = advance(*cur)
            nxt = (jnp.minimum(nxt[0], n - 1), jnp.minimum(nxt[1], n - 1))
            scores(nxt[0], nxt[1], 1 - parity)
            accumulate(prev[0], prev[1], 1 - parity)
            softmax(cur[0], parity, diagonal)
            return nxt, cur

        def body(_, carry):
            cur, prev = carry
            for u in range(DIFF_UNROLL):
                cur, prev = one(cur, prev, u % 2)
            return cur, prev

        cur, prev = lax.fori_loop(0, n_steps // DIFF_UNROLL, body, (first, first))
        for u in range(n_steps % DIFF_UNROLL):
            cur, prev = one(cur, prev, u % 2)
        accumulate(prev[0], prev[1], (n_steps - 1) % 2)

    def next_below_diagonal(i, j):
        wrap = j + 1 >= i
        return jnp.where(wrap, i + 1, i), jnp.where(wrap, 0, j + 1)

    if n > 1:
        run((jnp.int32(1), jnp.int32(0)), next_below_diagonal, n * (n - 1) // 2, False)
    run((jnp.int32(0), jnp.int32(0)), lambda i, j: (i + 1, j + 1), n, True)

    lam_vecs = lam_ref[...]
    lam = (jnp.exp(jnp.sum(lam_vecs[0:1] * lam_vecs[1:2], axis=-1, keepdims=True))
           - jnp.exp(jnp.sum(lam_vecs[2:3] * lam_vecs[3:4], axis=-1, keepdims=True))
           + lambda_init)

    def finish(t, c):
        ot = (acc_ref[t, 0, :dv] / acc_ref[t, 0, dv:dv + 1]
              - lam * (acc_ref[t, 1, :dv] / acc_ref[t, 1, dv:dv + 1]))
        o = _rmsnorm_rows(ot.T, g_ref[...]) * (1.0 - lambda_init)
        o_ref[0, pl.ds(pl.multiple_of(t * bq, bq), bq), :] = o.astype(o_ref.dtype)
        return c

    lax.fori_loop(0, n, finish, 0, unroll=TILE_LOOP_UNROLL)


def _diff_attention(proj, lam_vecs, subln_g, lambda_init):
    b, s, _ = proj.shape
    bq = DIFF_BLOCK
    heads = DIFF_HEADS
    n = s // bq
    rows = DIFF_V_DIM + ONES_ROWS
    return pl.pallas_call(
        functools.partial(_diff_kernel, lambda_init=lambda_init),
        grid=(b, heads),
        in_specs=[
            _whole(lam_vecs.shape),
            _whole((1, DIFF_V_DIM)),
            pl.BlockSpec((1, s, LANES), lambda bi, h: (bi, 0, h)),
            pl.BlockSpec((1, s, LANES), lambda bi, h: (bi, 0, heads + h)),
            pl.BlockSpec((1, s, LANES), lambda bi, h: (bi, 0, 2 * heads + h)),
        ],
        out_specs=pl.BlockSpec((1, s, LANES), lambda bi, h: (bi, 0, h)),
        out_shape=jax.ShapeDtypeStruct((b, s, heads * DIFF_V_DIM), BF16),
        scratch_shapes=[
            pltpu.VMEM((n, 2, LANES, bq), BF16),
            pltpu.VMEM((n, rows, bq), BF16),
            pltpu.VMEM((2, 2, bq, bq), F32),
            pltpu.VMEM((2, 2, bq, bq), BF16),
            pltpu.VMEM((2, 2, 1, bq), F32),
            pltpu.VMEM((n, 2, 1, bq), F32),
            pltpu.VMEM((n, 2, rows, bq), F32),
        ],
        compiler_params=pltpu.CompilerParams(
            dimension_semantics=("arbitrary", "arbitrary"), vmem_limit_bytes=VMEM_LIMIT),
        name="diff_attn",
    )(lam_vecs, subln_g.reshape(1, DIFF_V_DIM), proj, proj, proj)


def _post_kernel(*refs, n_mix, final_norm):
    x_ref = refs[0]
    mix_refs = refs[1:1 + n_mix]
    wo_ref, g_ref, wu_ref, wd_ref = refs[1 + n_mix:5 + n_mix]
    gf_ref = refs[5 + n_mix] if final_norm else None
    o_ref, h_ref = refs[-2:]

    o_ref[...] = x_ref[...]
    off = 0
    for mref in mix_refs:
        width = mref.shape[1]
        o_ref[...] += jnp.dot(mref[...], wo_ref[off:off + width, :], preferred_element_type=F32)
        off += width
    h_ref[...] = _rmsnorm_rows(o_ref[...], g_ref[...]).astype(BF16)
    d_ff = wu_ref.shape[1]
    for c in range(0, d_ff, FF_CHUNK):
        up = jnp.dot(h_ref[...], wu_ref[:, c:c + FF_CHUNK], preferred_element_type=F32)
        act = jnp.square(jnp.maximum(up, 0.0)).astype(BF16)
        o_ref[...] += jnp.dot(act, wd_ref[c:c + FF_CHUNK, :], preferred_element_type=F32)
    if final_norm:
        o_ref[...] = _rmsnorm_rows(o_ref[...], gf_ref[...])


def _post(x, mixes, w_out, g_mlp, w_up, w_down, g_final):
    m, d = x.shape
    d_ff = w_up.shape[1]
    tm = ROW_TILE
    final_norm = g_final is not None
    in_specs = [pl.BlockSpec((tm, d), lambda i: (i, 0))]
    in_specs += [pl.BlockSpec((tm, mx.shape[1]), lambda i: (i, 0)) for mx in mixes]
    in_specs += [_whole(w_out.shape), _whole((1, d)), _whole((d, d_ff)), _whole((d_ff, d))]
    args = [x, *mixes, w_out.astype(BF16), g_mlp.reshape(1, d), w_up.astype(BF16),
            w_down.astype(BF16)]
    if final_norm:
        in_specs.append(_whole((1, d)))
        args.append(g_final.reshape(1, d))
    return pl.pallas_call(
        functools.partial(_post_kernel, n_mix=len(mixes), final_norm=final_norm),
        grid=(m // tm,),
        in_specs=in_specs,
        out_specs=pl.BlockSpec((tm, d), lambda i: (i, 0)),
        out_shape=jax.ShapeDtypeStruct((m, d), F32),
        scratch_shapes=[pltpu.VMEM((tm, d), BF16)],
        compiler_params=pltpu.CompilerParams(
            dimension_semantics=("arbitrary",), vmem_limit_bytes=VMEM_LIMIT),
        name="out_proj_mlp",
    )(*args)


def kernel(x, norm_mix, norm_mlp, norm_final, w_in_even, conv_w, w_out_even, w_in_odd,
           lam_q1, lam_k1, lam_q2, lam_k2, subln_g, w_out_odd, w_up, w_down):
    b, s, d = x.shape
    m = b * s
    depth = norm_mix.shape[0]
    xf = x.reshape(m, d)
    for layer in range(depth):
        i = layer // 2
        g_final = norm_final if layer == depth - 1 else None
        if layer % 2 == 0:
            qkv, c_out = _in_proj_even(xf, norm_mix[layer], w_in_even[i], conv_w[i], s)
            a_out = _sb_attention(qkv.reshape(b, s, 3 * SB_WIDTH))
            mixes = [a_out.reshape(m, SB_WIDTH), c_out]
            w_out = w_out_even[i]
        else:
            proj = _in_proj_odd(xf, norm_mix[layer], w_in_odd[i])
            lambda_init = 0.8 - 0.6 * math.exp(-0.3 * layer)
            lam_vecs = jnp.stack([lam_q1[i], lam_k1[i], lam_q2[i], lam_k2[i]])
            o = _diff_attention(proj.reshape(b, s, proj.shape[1]), lam_vecs, subln_g[i],
                                lambda_init)
            mixes = [o.reshape(m, DIFF_HEADS * DIFF_V_DIM)]
            w_out = w_out_odd[i]
        xf = _post(xf, mixes, w_out, norm_mlp[layer], w_up[layer], w_down[layer], g_final)
    return xf.reshape(b, s, d)
```

```python
import functools
import math

import jax
import jax.numpy as jnp
from jax import lax
from jax.experimental import pallas as pl
from jax.experimental.pallas import tpu as pltpu

F32 = jnp.float32
BF16 = jnp.bfloat16

NORM_EPS = 1e-6
HEAD_DIM = 64
SB_HEADS = 8
SB_WIDTH = SB_HEADS * HEAD_DIM
CONV_K = 3
DIFF_HEADS = 8
DIFF_V_DIM = 2 * HEAD_DIM
QK_SCALE = HEAD_DIM ** -0.5
LOG2E = math.log2(math.e)
ONES_ROWS = 16

LANES = 128
SUBLANES = 8
VMEM_LIMIT = 48 * 1024 * 1024

ROW_TILE = 512
IN_PROJ_ROW_TILE = 1024
FF_CHUNK = 1024
SB_BLOCK = 128
SB_TILES_PER_STEP = 2
DIFF_BLOCK = 256
DIFF_UNROLL = 32
TILE_LOOP_UNROLL = 8

SB_SKIP_LOG_MASS = 50.0

MASK_VALUE = -1e30


def _rmsnorm_rows(x, g):
    ms = jnp.mean(x * x, axis=-1, keepdims=True)
    return x * lax.rsqrt(ms + NORM_EPS) * g


def _whole(shape):
    return pl.BlockSpec(shape, lambda *_: (0,) * len(shape), pipeline_mode=pl.Buffered(1))


def _in_proj_even_kernel(x_ref, g_ref, w_ref, cw_ref, qkv_ref, c_ref, cu_ref, *, tiles_per_seq):
    tm = x_ref.shape[0]
    cw = c_ref.shape[1]

    @pl.when(pl.program_id(0) % tiles_per_seq == 0)
    def _():
        cu_ref[:SUBLANES, :] = jnp.zeros((SUBLANES, cw), F32)

    h = _rmsnorm_rows(x_ref[...], g_ref[...]).astype(BF16)
    base = 3 * SB_WIDTH
    c_gate = jnp.dot(h, w_ref[:, base + cw:base + 2 * cw], preferred_element_type=F32)
    u = jnp.dot(h, w_ref[:, base + 2 * cw:], preferred_element_type=F32)
    cu_ref[SUBLANES:, :] = c_gate * u
    y = cw_ref[0:1, :] * cu_ref[SUBLANES - 2:SUBLANES - 2 + tm, :]
    y = y + cw_ref[1:2, :] * cu_ref[SUBLANES - 1:SUBLANES - 1 + tm, :]
    y = y + cw_ref[2:3, :] * cu_ref[SUBLANES:, :]
    b_gate = jnp.dot(h, w_ref[:, base:base + cw], preferred_element_type=F32)
    c_ref[...] = (b_gate * y).astype(BF16)
    cu_ref[:SUBLANES, :] = cu_ref[tm:, :]
    q = jnp.dot(h, w_ref[:, :SB_WIDTH], preferred_element_type=F32) * (QK_SCALE * LOG2E)
    qkv_ref[:, :SB_WIDTH] = q.astype(BF16)
    kv = jnp.dot(h, w_ref[:, SB_WIDTH:3 * SB_WIDTH], preferred_element_type=F32)
    qkv_ref[:, SB_WIDTH:] = kv.astype(BF16)


def _in_proj_even(x, g, w, conv_w, seq):
    m, d = x.shape
    n = w.shape[1]
    cw = conv_w.shape[1]
    tm = IN_PROJ_ROW_TILE
    assert seq % tm == 0, "a sequence must hold a whole number of row tiles (conv halo carry)"
    return pl.pallas_call(
        functools.partial(_in_proj_even_kernel, tiles_per_seq=seq // tm),
        grid=(m // tm,),
        in_specs=[
            pl.BlockSpec((tm, d), lambda i: (i, 0)),
            _whole((1, d)),
            _whole((d, n)),
            _whole((CONV_K, cw)),
        ],
        out_specs=[
            pl.BlockSpec((tm, 3 * SB_WIDTH), lambda i: (i, 0)),
            pl.BlockSpec((tm, cw), lambda i: (i, 0)),
        ],
        out_shape=[
            jax.ShapeDtypeStruct((m, 3 * SB_WIDTH), BF16),
            jax.ShapeDtypeStruct((m, cw), BF16),
        ],
        scratch_shapes=[pltpu.VMEM((tm + SUBLANES, cw), F32)],
        compiler_params=pltpu.CompilerParams(
            dimension_semantics=("arbitrary",), vmem_limit_bytes=VMEM_LIMIT),
        name="in_proj_even",
    )(x, g.reshape(1, d), w.astype(BF16), conv_w)


def _in_proj_odd_kernel(x_ref, g_ref, w_ref, o_ref, *, q_width):
    h = _rmsnorm_rows(x_ref[...], g_ref[...]).astype(BF16)
    q = jnp.dot(h, w_ref[:, :q_width], preferred_element_type=F32) * (QK_SCALE * LOG2E)
    o_ref[:, :q_width] = q.astype(BF16)
    kv = jnp.dot(h, w_ref[:, q_width:], preferred_element_type=F32)
    o_ref[:, q_width:] = kv.astype(BF16)


def _in_proj_odd(x, g, w):
    m, d = x.shape
    n = w.shape[1]
    tm = IN_PROJ_ROW_TILE
    assert m % tm == 0
    return pl.pallas_call(
        functools.partial(_in_proj_odd_kernel, q_width=DIFF_HEADS * 2 * HEAD_DIM),
        grid=(m // tm,),
        in_specs=[
            pl.BlockSpec((tm, d), lambda i: (i, 0)),
            _whole((1, d)),
            _whole((d, n)),
        ],
        out_specs=pl.BlockSpec((tm, n), lambda i: (i, 0)),
        out_shape=jax.ShapeDtypeStruct((m, n), BF16),
        compiler_params=pltpu.CompilerParams(
            dimension_semantics=("arbitrary",), vmem_limit_bytes=VMEM_LIMIT),
        name="in_proj_odd",
    )(x, g.reshape(1, d), w.astype(BF16))


def _sb_kernel(q_ref, k_ref, v_ref, tri_ref, o_ref, vt_ref, qt_ref, z_ref, hl_ref, w_ref,
               carry_ref, acc_ref):
    bq = SB_BLOCK
    bk = bq
    n_tiles = q_ref.shape[1] // bq
    n_pairs, n_blocks = vt_ref.shape[:2]
    step = pl.program_id(1)
    chains = [(u, p) for u in range(n_tiles) for p in range(n_pairs)]

    def pair_cols(p):
        return slice(p * LANES, (p + 1) * LANES)

    @pl.when(step == 0)
    def _():
        def tr(j, c):
            off = pl.multiple_of(j * bk, bk)
            for p in range(n_pairs):
                vj = v_ref[0, pl.ds(off, bk), pair_cols(p)]
                vt_ref[p, j] = vj.astype(F32).T.astype(BF16)
            return c
        lax.fori_loop(0, n_blocks, tr, 0)

    lo_half = lax.broadcasted_iota(jnp.int32, (LANES, bq), 0) < HEAD_DIM
    for u, p in chains:
        qt = q_ref[0, u * bq:(u + 1) * bq, pair_cols(p)].astype(F32).T
        qt_ref[u, p, :, :bq] = jnp.where(lo_half, qt, 0.0).astype(BF16)
        qt_ref[u, p, :, bq:] = jnp.where(lo_half, 0.0, qt).astype(BF16)

    tri = tri_ref[...]

    def softplus(zt):
        return jnp.maximum(zt, 0.0) + jnp.log(1.0 + jnp.exp2(-jnp.abs(zt))) * LOG2E

    def split_bf16(sp):
        hi = sp.astype(BF16)
        return hi, (sp - hi.astype(F32)).astype(BF16)

    tile = [step * n_tiles + u for u in range(n_tiles)]
    first = [jnp.maximum(i - 1, 0) for i in tile]
    key = lax.broadcasted_iota(jnp.int32, (2 * bk, 2 * bq), 0)
    qry = lax.broadcasted_iota(jnp.int32, (2 * bk, 2 * bq), 1) % bq
    bias = [jnp.where(key - qry < (tile[u] - first[u]) * bq, 0.0, MASK_VALUE)
            for u in range(n_tiles)]
    for u, p in chains:
        k2 = k_ref[0, pl.ds(pl.multiple_of(first[u] * bk, bk), 2 * bk), pair_cols(p)]
        zt = jnp.dot(k2, qt_ref[u, p], preferred_element_type=F32)
        zt = zt + bias[u]
        z_ref[u, p] = zt
        hi, lo = split_bf16(softplus(zt))
        for half in range(2):
            rows = slice(half * bk, (half + 1) * bk)
            hl_ref[u, p, half] = jnp.concatenate([hi[rows], lo[rows]], axis=0)
    for u, p in chains:
        cs_new = jnp.dot(tri, hl_ref[u, p, 1], preferred_element_type=F32)
        cs_old = jnp.dot(tri, hl_ref[u, p, 0], preferred_element_type=F32)
        total_new = cs_new[bk:bk + 1]
        w_ref[u, p, 1] = jnp.exp2(z_ref[u, p, bk:, :] - cs_new[:bk]).astype(BF16)
        w_ref[u, p, 0] = jnp.exp2(z_ref[u, p, :bk, :] - cs_old[:bk] - total_new).astype(BF16)
        carry_ref[u, p] = total_new + cs_old[bk:bk + 1]
    for u, p in chains:
        acc_ref[u, p] = (
            jnp.dot(vt_ref[p, first[u]], w_ref[u, p, 0], preferred_element_type=F32)
            + jnp.dot(vt_ref[p, first[u] + 1], w_ref[u, p, 1], preferred_element_type=F32))

    for u in range(n_tiles):
        def least_carry(u=u):
            lows = [jnp.min(carry_ref[u, p]) for p in range(n_pairs)]
            return functools.reduce(jnp.minimum, lows)

        def cond(state):
            j, go = state
            return jnp.logical_and(j >= 0, go)

        def body(state, u=u, least_carry=least_carry):
            j, _ = state
            off = pl.multiple_of(j * bk, bk)
            for p in range(n_pairs):
                kj = k_ref[0, pl.ds(off, bk), pair_cols(p)]
                zt = jnp.dot(kj, qt_ref[u, p], preferred_element_type=F32)
                hi, lo = split_bf16(softplus(zt))
                cs = jnp.dot(tri, jnp.concatenate([hi, lo], axis=0),
                             preferred_element_type=F32)
                wt = jnp.exp2(zt - cs[:bk] - carry_ref[u, p]).astype(BF16)
                acc_ref[u, p] += jnp.dot(vt_ref[p, j], wt, preferred_element_type=F32)
                carry_ref[u, p] += cs[bk:bk + 1]
            return j - 1, least_carry() < SB_SKIP_LOG_MASS * LOG2E

        lax.while_loop(cond, body, (first[u] - 1, least_carry() < SB_SKIP_LOG_MASS * LOG2E))

    feat = lax.broadcasted_iota(jnp.int32, (LANES, bq), 0)
    for u, p in chains:
        res = acc_ref[u, p]
        ot = jnp.where(feat < HEAD_DIM, res[:, :bq], res[:, bq:])
        o_ref[0, u * bq:(u + 1) * bq, pair_cols(p)] = ot.T.astype(o_ref.dtype)


def _suffix_sum_matrix(bk):
    s = lax.broadcasted_iota(jnp.int32, (bk + ONES_ROWS, 2 * bk), 0)
    j = lax.broadcasted_iota(jnp.int32, (bk + ONES_ROWS, 2 * bk), 1) % bk
    return jnp.logical_or(s >= bk, j >= s).astype(BF16)


def _sb_attention(qkv):
    b, s, _ = qkv.shape
    bq = SB_BLOCK
    tiles = SB_TILES_PER_STEP
    assert s % (tiles * bq) == 0 and s >= 2 * bq
    pairs = SB_WIDTH // LANES
    tri = _suffix_sum_matrix(bq)
    return pl.pallas_call(
        _sb_kernel,
        grid=(b, s // (tiles * bq)),
        in_specs=[
            pl.BlockSpec((1, tiles * bq, SB_WIDTH), lambda bi, i: (bi, i, 0)),
            pl.BlockSpec((1, s, SB_WIDTH), lambda bi, i: (bi, 0, 1), pipeline_mode=pl.Buffered(1)),
            pl.BlockSpec((1, s, SB_WIDTH), lambda bi, i: (bi, 0, 2), pipeline_mode=pl.Buffered(1)),
            _whole(tri.shape),
        ],
        out_specs=pl.BlockSpec((1, tiles * bq, SB_WIDTH), lambda bi, i: (bi, i, 0)),
        out_shape=jax.ShapeDtypeStruct((b, s, SB_WIDTH), BF16),
        scratch_shapes=[
            pltpu.VMEM((pairs, s // bq, LANES, bq), BF16),
            pltpu.VMEM((tiles, pairs, LANES, 2 * bq), BF16),
            pltpu.VMEM((tiles, pairs, 2 * bq, 2 * bq), F32),
            pltpu.VMEM((tiles, pairs, 2, 2 * bq, 2 * bq), BF16),
            pltpu.VMEM((tiles, pairs, 2, bq, 2 * bq), BF16),
            pltpu.VMEM((tiles, pairs, 1, 2 * bq), F32),
            pltpu.VMEM((tiles, pairs, LANES, 2 * bq), F32),
        ],
        compiler_params=pltpu.CompilerParams(
            dimension_semantics=("arbitrary", "arbitrary"), vmem_limit_bytes=VMEM_LIMIT),
        name="stickbreak_attn",
    )(qkv, qkv, qkv, tri)


def _diff_kernel(lam_ref, g_ref, q_ref, k_ref, v_ref, o_ref, qt_ref, vt_ref, s_ref, p_ref, a_ref,
                 m_ref, acc_ref, *, lambda_init):
    bq = DIFF_BLOCK
    bk = bq
    n = q_ref.shape[1] // bq
    dv = DIFF_V_DIM
    lo_half = lax.broadcasted_iota(jnp.int32, (LANES, bq), 0) < HEAD_DIM

    def prepare(t, c):
        rows = pl.ds(pl.multiple_of(t * bq, bq), bq)
        qt = q_ref[0, rows, :].astype(F32).T
        qt_ref[t, 0] = jnp.where(lo_half, qt, 0.0).astype(BF16)
        qt_ref[t, 1] = jnp.where(lo_half, 0.0, qt).astype(BF16)
        vt_ref[t, :dv, :] = v_ref[0, rows, :].astype(F32).T.astype(BF16)
        vt_ref[t, dv:, :] = jnp.ones((ONES_ROWS, bk), BF16)
        m_ref[t] = jnp.full(m_ref.shape[1:], MASK_VALUE, F32)
        acc_ref[t] = jnp.zeros(acc_ref.shape[1:], F32)
        return c

    lax.fori_loop(0, n, prepare, 0, unroll=TILE_LOOP_UNROLL)

    def scores(i, j, slot):
        kj = k_ref[0, pl.ds(pl.multiple_of(j * bk, bk), bk), :]
        for mp in range(2):
            s_ref[slot, mp] = jnp.dot(kj, qt_ref[i, mp], preferred_element_type=F32)

    def softmax(i, slot, diagonal):
        for mp in range(2):
            st = s_ref[slot, mp]
            if diagonal:
                key = lax.broadcasted_iota(jnp.int32, (bk, bq), 0)
                qry = lax.broadcasted_iota(jnp.int32, (bk, bq), 1)
                st = jnp.where(key <= qry, st, MASK_VALUE)
            m_prev = m_ref[i, mp]
            m_new = jnp.maximum(m_prev, jnp.max(st, axis=0, keepdims=True))
            a_ref[slot, mp] = jnp.exp2(m_prev - m_new)
            p_ref[slot, mp] = jnp.exp2(st - m_new).astype(BF16)
            m_ref[i, mp] = m_new

    def accumulate(i, j, slot):
        for mp in range(2):
            pv = jnp.dot(vt_ref[j], p_ref[slot, mp], preferred_element_type=F32)
            acc_ref[i, mp] = a_ref[slot, mp] * acc_ref[i, mp] + pv

    def run(first, advance, n_steps, diagonal):
        scores(first[0], first[1], 0)
        p_ref[1] = jnp.zeros(p_ref.shape[1:], BF16)
        a_ref[1] = jnp.ones(a_ref.shape[1:], F32)

        def one(cur, prev, parity):
            nxt = advance(*cur)
            nxt = (jnp.minimum(nxt[0], n - 1), jnp.minimum(nxt[1], n - 1))
            scores(nxt[0], nxt[1], 1 - parity)
            accumulate(prev[0], prev[1], 1 - parity)
            softmax(cur[0], parity, diagonal)
            return nxt, cur

        def body(_, carry):
            cur, prev = carry
            for u in range(DIFF_UNROLL):
                cur, prev = one(cur, prev, u % 2)
            return cur, prev

        cur, prev = lax.fori_loop(0, n_steps // DIFF_UNROLL, body, (first, first))
        for u in range(n_steps % DIFF_UNROLL):
            cur, prev = one(cur, prev, u % 2)
        accumulate(prev[0], prev[1], (n_steps - 1) % 2)

    def next_below_diagonal(i, j):
        wrap = j + 1 >= i
        return jnp.where(wrap, i + 1, i), jnp.where(wrap, 0, j + 1)

    if n > 1:
        run((jnp.int32(1), jnp.int32(0)), next_below_diagonal, n * (n - 1) // 2, False)
    run((jnp.int32(0), jnp.int32(0)), lambda i, j: (i + 1, j + 1), n, True)

    lam_vecs = lam_ref[...]
    lam = (jnp.exp(jnp.sum(lam_vecs[0:1] * lam_vecs[1:2], axis=-1, keepdims=True))
           - jnp.exp(jnp.sum(lam_vecs[2:3] * lam_vecs[3:4], axis=-1, keepdims=True))
           + lambda_init)

    def finish(t, c):
        inv1 = 1.0 / acc_ref[t, 0, dv:dv + 1]
        inv2 = lam / acc_ref[t, 1, dv:dv + 1]
        ot = acc_ref[t, 0, :dv] * inv1 - acc_ref[t, 1, :dv] * inv2
        ms = jnp.mean(ot * ot, axis=0, keepdims=True)
        o = ot * lax.rsqrt(ms + NORM_EPS) * g_ref[...] * (1.0 - lambda_init)
        o_ref[0, pl.ds(pl.multiple_of(t * bq, bq), bq), :] = o.T.astype(o_ref.dtype)
        return c

    lax.fori_loop(0, n, finish, 0, unroll=TILE_LOOP_UNROLL)


def _diff_attention(proj, lam_vecs, subln_g, lambda_init):
    b, s, _ = proj.shape
    bq = DIFF_BLOCK
    heads = DIFF_HEADS
    assert s % bq == 0
    n = s // bq
    rows = DIFF_V_DIM + ONES_ROWS
    return pl.pallas_call(
        functools.partial(_diff_kernel, lambda_init=lambda_init),
        grid=(b, heads),
        in_specs=[
            _whole(lam_vecs.shape),
            _whole((DIFF_V_DIM, bq)),
            pl.BlockSpec((1, s, LANES), lambda bi, h: (bi, 0, h)),
            pl.BlockSpec((1, s, LANES), lambda bi, h: (bi, 0, heads + h)),
            pl.BlockSpec((1, s, LANES), lambda bi, h: (bi, 0, 2 * heads + h)),
        ],
        out_specs=pl.BlockSpec((1, s, LANES), lambda bi, h: (bi, 0, h)),
        out_shape=jax.ShapeDtypeStruct((b, s, heads * DIFF_V_DIM), BF16),
        scratch_shapes=[
            pltpu.VMEM((n, 2, LANES, bq), BF16),
            pltpu.VMEM((n, rows, bq), BF16),
            pltpu.VMEM((2, 2, bq, bq), F32),
            pltpu.VMEM((2, 2, bq, bq), BF16),
            pltpu.VMEM((2, 2, 1, bq), F32),
            pltpu.VMEM((n, 2, 1, bq), F32),
            pltpu.VMEM((n, 2, rows, bq), F32),
        ],
        compiler_params=pltpu.CompilerParams(
            dimension_semantics=("arbitrary", "arbitrary"), vmem_limit_bytes=VMEM_LIMIT),
        name="diff_attn",
    )(lam_vecs, jnp.broadcast_to(subln_g.reshape(DIFF_V_DIM, 1), (DIFF_V_DIM, bq)),
      proj, proj, proj)


def _post_kernel(*refs, n_mix, final_norm):
    x_ref = refs[0]
    mix_refs = refs[1:1 + n_mix]
    wo_ref, g_ref, wu_ref, wd_ref = refs[1 + n_mix:5 + n_mix]
    gf_ref = refs[5 + n_mix] if final_norm else None
    o_ref, h_ref = refs[-2:]

    o_ref[...] = x_ref[...]
    off = 0
    for mref in mix_refs:
        width = mref.shape[1]
        o_ref[...] += jnp.dot(mref[...], wo_ref[off:off + width, :], preferred_element_type=F32)
        off += width
    h_ref[...] = _rmsnorm_rows(o_ref[...], g_ref[...]).astype(BF16)
    d_ff = wu_ref.shape[1]
    for c in range(0, d_ff, FF_CHUNK):
        up = jnp.dot(h_ref[...], wu_ref[:, c:c + FF_CHUNK], preferred_element_type=F32)
        act = jnp.square(jnp.maximum(up, 0.0)).astype(BF16)
        o_ref[...] += jnp.dot(act, wd_ref[c:c + FF_CHUNK, :], preferred_element_type=F32)
    if final_norm:
        o_ref[...] = _rmsnorm_rows(o_ref[...], gf_ref[...])


def _post(x, mixes, w_out, g_mlp, w_up, w_down, g_final):
    m, d = x.shape
    d_ff = w_up.shape[1]
    tm = ROW_TILE
    assert m % tm == 0 and d_ff % FF_CHUNK == 0
    final_norm = g_final is not None
    in_specs = [pl.BlockSpec((tm, d), lambda i: (i, 0))]
    in_specs += [pl.BlockSpec((tm, mx.shape[1]), lambda i: (i, 0)) for mx in mixes]
    in_specs += [_whole(w_out.shape), _whole((1, d)), _whole((d, d_ff)), _whole((d_ff, d))]
    args = [x, *mixes, w_out.astype(BF16), g_mlp.reshape(1, d), w_up.astype(BF16),
            w_down.astype(BF16)]
    if final_norm:
        in_specs.append(_whole((1, d)))
        args.append(g_final.reshape(1, d))
    return pl.pallas_call(
        functools.partial(_post_kernel, n_mix=len(mixes), final_norm=final_norm),
        grid=(m // tm,),
        in_specs=in_specs,
        out_specs=pl.BlockSpec((tm, d), lambda i: (i, 0)),
        out_shape=jax.ShapeDtypeStruct((m, d), F32),
        scratch_shapes=[pltpu.VMEM((tm, d), BF16)],
        compiler_params=pltpu.CompilerParams(
            dimension_semantics=("arbitrary",), vmem_limit_bytes=VMEM_LIMIT),
        name="out_proj_mlp",
    )(*args)


def kernel(x, norm_mix, norm_mlp, norm_final, w_in_even, conv_w, w_out_even, w_in_odd,
           lam_q1, lam_k1, lam_q2, lam_k2, subln_g, w_out_odd, w_up, w_down):
    b, s, d = x.shape
    m = b * s
    depth = norm_mix.shape[0]
    xf = x.reshape(m, d)
    for layer in range(depth):
        i = layer // 2
        g_final = norm_final if layer == depth - 1 else None
        if layer % 2 == 0:
            qkv, c_out = _in_proj_even(xf, norm_mix[layer], w_in_even[i], conv_w[i], s)
            a_out = _sb_attention(qkv.reshape(b, s, 3 * SB_WIDTH))
            mixes = [a_out.reshape(m, SB_WIDTH), c_out]
            w_out = w_out_even[i]
        else:
            proj = _in_proj_odd(xf, norm_mix[layer], w_in_odd[i])
            lambda_init = 0.8 - 0.6 * math.exp(-0.3 * layer)
            lam_vecs = jnp.stack([lam_q1[i], lam_k1[i], lam_q2[i], lam_k2[i]])
            o = _diff_attention(proj.reshape(b, s, proj.shape[1]), lam_vecs, subln_g[i],
                                lambda_init)
            mixes = [o.reshape(m, DIFF_HEADS * DIFF_V_DIM)]
            w_out = w_out_odd[i]
        xf = _post(xf, mixes, w_out, norm_mlp[layer], w_up[layer], w_down[layer], g_final)
    return xf.reshape(b, s, d)
```
